```python
import math
import jax, jax.numpy as jnp
from jax import lax
import numpy as np

D_MODEL = 2048
BATCH = 16
SEQ = 2048
DEPTH = 4

N_MIXERS = 2
BLOCK_Q = 128
SB_HEADS = 16
SB_HEAD_DIM = D_MODEL // SB_HEADS
MLA_HEADS = 16
Q_LORA_RANK = 512
KV_LORA_RANK = 512
QK_NOPE_DIM = 128
QK_ROPE_DIM = 64
V_HEAD_DIM = 128
ROPE_THETA = 10000.0
MLA_IN_DIM = Q_LORA_RANK + KV_LORA_RANK + QK_ROPE_DIM
D_FF = 3 * D_MODEL
N_EXPERTS = 8
TOP_K = 2
N_MOD = 6
DEEPNORM_ALPHA = (2 * DEPTH) ** 0.25
DEEPNORM_BETA = (8 * DEPTH) ** -0.25
LN_EPS = 1e-5
RMS_EPS = 1e-6
N_SB_LAYERS = (DEPTH + 1) // 2
N_MLA_LAYERS = DEPTH // 2

kernel_name = "stickbreak_mla_moe_deepnorm_adaln_trunk"


def layer_norm(x, g, b):
    xf = x.astype(jnp.float32)
    mu = jnp.mean(xf, axis=-1, keepdims=True)
    var = jnp.mean(jnp.square(xf - mu), axis=-1, keepdims=True)
    y = (xf - mu) * lax.rsqrt(var + LN_EPS)
    return (y * g.astype(jnp.float32) + b.astype(jnp.float32)).astype(x.dtype)


def rms_norm(x, g):
    xf = x.astype(jnp.float32)
    y = xf * lax.rsqrt(jnp.mean(jnp.square(xf), axis=-1, keepdims=True) + RMS_EPS)
    return (y * g.astype(jnp.float32)).astype(x.dtype)


def apply_rope(x, cos, sin):
    xf = x.astype(jnp.float32)
    x1, x2 = jnp.split(xf, 2, axis=-1)
    return jnp.concatenate([x1 * cos - x2 * sin, x1 * sin + x2 * cos], axis=-1).astype(x.dtype)


def stick_breaking_attention(h, w_qkv, w_o):
    b, s, _ = h.shape
    qkv = jnp.einsum('bsd,de->bse', h, w_qkv).reshape(b, s, 3, SB_HEADS, SB_HEAD_DIM)
    q, k, v = qkv[:, :, 0], qkv[:, :, 1], qkv[:, :, 2]
    scale = SB_HEAD_DIM ** -0.5
    outs = []
    for q0 in range(0, s, BLOCK_Q):
        k_end = q0 + BLOCK_Q
        z = jnp.einsum('bqhd,bkhd->bhqk', q[:, q0:k_end], k[:, :k_end]).astype(jnp.float32) * scale
        t_idx = q0 + jnp.arange(BLOCK_Q)[:, None]
        s_idx = jnp.arange(k_end)[None, :]
        strict = s_idx < t_idx
        log_one_minus_beta = jnp.where(strict, -jax.nn.softplus(z), 0.0)
        tail = lax.cumsum(log_one_minus_beta, axis=3, reverse=True) - log_one_minus_beta
        log_a = jax.nn.log_sigmoid(z) + tail
        a = jnp.where(strict, jnp.exp(log_a), 0.0)
        outs.append(jnp.einsum('bhqk,bkhd->bqhd', a.astype(v.dtype), v[:, :k_end]))
    o = jnp.concatenate(outs, axis=1).reshape(b, s, SB_HEADS * SB_HEAD_DIM)
    return jnp.einsum('bse,ed->bsd', o, w_o)


def mla_attention(h, cos, sin, w_in, q_norm, kv_norm, w_uq, w_ukv, w_o):
    b, s, _ = h.shape
    lat = jnp.einsum('bsd,de->bse', h, w_in)
    c_q, c_kv, k_r = jnp.split(lat, [Q_LORA_RANK, Q_LORA_RANK + KV_LORA_RANK], axis=-1)
    q = jnp.einsum('bsr,re->bse', rms_norm(c_q, q_norm), w_uq).reshape(
        b, s, MLA_HEADS, QK_NOPE_DIM + QK_ROPE_DIM)
    q_nope, q_rope = jnp.split(q, [QK_NOPE_DIM], axis=-1)
    q_rope = apply_rope(q_rope, cos, sin)
    k_rope = apply_rope(k_r[:, :, None, :], cos, sin)[:, :, 0]
    kv = jnp.einsum('bsr,re->bse', rms_norm(c_kv, kv_norm), w_ukv).reshape(
        b, s, MLA_HEADS, QK_NOPE_DIM + V_HEAD_DIM)
    k_nope, v = jnp.split(kv, [QK_NOPE_DIM], axis=-1)
    scale = (QK_NOPE_DIM + QK_ROPE_DIM) ** -0.5
    outs = []
    for q0 in range(0, s, BLOCK_Q):
        k_end = q0 + BLOCK_Q
        sc = (jnp.einsum('bqhd,bkhd->bhqk', q_nope[:, q0:k_end], k_nope[:, :k_end])
              + jnp.einsum('bqhr,bkr->bhqk', q_rope[:, q0:k_end], k_rope[:, :k_end])
              ).astype(jnp.float32) * scale
        causal = jnp.arange(k_end)[None, :] <= (q0 + jnp.arange(BLOCK_Q))[:, None]
        p = jax.nn.softmax(jnp.where(causal, sc, -jnp.inf), axis=-1)
        outs.append(jnp.einsum('bhqk,bkhd->bqhd', p.astype(v.dtype), v[:, :k_end]))
    o = jnp.concatenate(outs, axis=1).reshape(b, s, MLA_HEADS * V_HEAD_DIM)
    return jnp.einsum('bse,ed->bsd', o, w_o)


def swiglu(h, w_gu, w_down):
    g, u = jnp.split(jnp.einsum('...d,df->...f', h, w_gu), 2, axis=-1)
    return jnp.einsum('...f,fd->...d', jax.nn.silu(g) * u, w_down)


def moe_swiglu(h, w_router, w_gu, w_down):
    logits = jnp.einsum('bsd,de->bse', h, w_router).astype(jnp.float32)
    top_logits, top_idx = lax.top_k(logits, TOP_K)
    top_w = jax.nn.softmax(top_logits, axis=-1)
    gates = jnp.sum(jax.nn.one_hot(top_idx, N_EXPERTS, dtype=jnp.float32) * top_w[..., None], axis=-2)
    out = jnp.zeros_like(h)
    for e in range(N_EXPERTS):
        out = out + gates[..., e:e + 1].astype(h.dtype) * swiglu(h, w_gu[e], w_down[e])
    return out


def setup_inputs(seed: int = 0) -> dict:
    key = jax.random.key(seed)
    ks = jax.random.split(key, 24)
    f32 = jnp.float32
    d = D_MODEL
    nrm = lambda k, shape, s: jax.random.normal(k, shape, f32) * s
    offset = jax.random.randint(ks[2], (BATCH, 1), 0, 1024, dtype=jnp.int32)
    positions = offset + jnp.arange(SEQ, dtype=jnp.int32)[None, :]
    return {
        "x": nrm(ks[0], (BATCH, SEQ, d), 1.0),
        "c": nrm(ks[1], (BATCH, d), 1.0),
        "positions": positions,
        "mod_w": nrm(ks[3], (DEPTH, d, N_MOD * d), 0.1 * d ** -0.5),
        "mod_b": nrm(ks[4], (DEPTH, N_MOD * d), 0.01),
        "ln_g": 1.0 + nrm(ks[5], (DEPTH, 2, d), 0.01),
        "ln_b": nrm(ks[6], (DEPTH, 2, d), 0.01),
        "sb_w_qkv": nrm(ks[7], (N_SB_LAYERS, d, 3 * SB_HEADS * SB_HEAD_DIM), d ** -0.5),
        "sb_w_o": nrm(ks[8], (N_SB_LAYERS, SB_HEADS * SB_HEAD_DIM, d), DEEPNORM_BETA * (SB_HEADS * SB_HEAD_DIM) ** -0.5),
        "mla_w_in": nrm(ks[9], (N_MLA_LAYERS, d, MLA_IN_DIM), d ** -0.5),
        "mla_q_norm": 1.0 + nrm(ks[10], (N_MLA_LAYERS, Q_LORA_RANK), 0.01),
        "mla_kv_norm": 1.0 + nrm(ks[11], (N_MLA_LAYERS, KV_LORA_RANK), 0.01),
        "mla_w_uq": nrm(ks[12], (N_MLA_LAYERS, Q_LORA_RANK, MLA_HEADS * (QK_NOPE_DIM + QK_ROPE_DIM)), Q_LORA_RANK ** -0.5),
        "mla_w_ukv": nrm(ks[13], (N_MLA_LAYERS, KV_LORA_RANK, MLA_HEADS * (QK_NOPE_DIM + V_HEAD_DIM)), KV_LORA_RANK ** -0.5),
        "mla_w_o": nrm(ks[14], (N_MLA_LAYERS, MLA_HEADS * V_HEAD_DIM, d), DEEPNORM_BETA * (MLA_HEADS * V_HEAD_DIM) ** -0.5),
        "ffn_w_gu": nrm(ks[15], (N_SB_LAYERS, d, 2 * D_FF), d ** -0.5),
        "ffn_w_down": nrm(ks[16], (N_SB_LAYERS, D_FF, d), DEEPNORM_BETA * D_FF ** -0.5),
        "moe_router": nrm(ks[17], (N_MLA_LAYERS, d, N_EXPERTS), d ** -0.5),
        "moe_w_gu": nrm(ks[18], (N_MLA_LAYERS, N_EXPERTS, d, 2 * D_FF), d ** -0.5),
        "moe_w_down": nrm(ks[19], (N_MLA_LAYERS, N_EXPERTS, D_FF, d), DEEPNORM_BETA * D_FF ** -0.5),
    }


def reference(x, c, positions, mod_w, mod_b, ln_g, ln_b, sb_w_qkv, sb_w_o,
              mla_w_in, mla_q_norm, mla_kv_norm, mla_w_uq, mla_w_ukv, mla_w_o,
              ffn_w_gu, ffn_w_down, moe_router, moe_w_gu, moe_w_down):
    half = QK_ROPE_DIM // 2
    inv_freq = ROPE_THETA ** (-jnp.arange(half, dtype=jnp.float32) / half)
    ang = positions.astype(jnp.float32)[:, :, None, None] * inv_freq
    cos, sin = jnp.cos(ang), jnp.sin(ang)
    cond = jax.nn.silu(c)
    for i in range(DEPTH):
        j = i // 2
        mod = (jnp.einsum('bd,de->be', cond, mod_w[i]) + mod_b[i])[:, None, :]
        sh_a, sc_a, g_a, sh_f, sc_f, g_f = jnp.split(mod, N_MOD, axis=-1)
        h = x * (1.0 + sc_a) + sh_a
        if i % N_MIXERS == 0:
            y = stick_breaking_attention(h, sb_w_qkv[j], sb_w_o[j])
        else:
            y = mla_attention(h, cos, sin, mla_w_in[j], mla_q_norm[j], mla_kv_norm[j],
                              mla_w_uq[j], mla_w_ukv[j], mla_w_o[j])
        x = layer_norm(DEEPNORM_ALPHA * x + (1.0 + g_a) * y, ln_g[i, 0], ln_b[i, 0])
        h = x * (1.0 + sc_f) + sh_f
        if i % 2 == 0:
            y = swiglu(h, ffn_w_gu[j], ffn_w_down[j])
        else:
            y = moe_swiglu(h, moe_router[j], moe_w_gu[j], moe_w_down[j])
        x = layer_norm(DEEPNORM_ALPHA * x + (1.0 + g_f) * y, ln_g[i, 1], ln_b[i, 1])
    return x
```

```python
import functools

import jax
import jax.numpy as jnp
from jax import lax
from jax.experimental import pallas as pl
from jax.experimental.pallas import tpu as pltpu

F32 = jnp.float32
BF16 = jnp.bfloat16

HEAD_DIM = 128
ROPE_DIM = 64
ROPE_HALF = ROPE_DIM // 2
ROPE_THETA = 10000.0
N_MOD = 6
TOP_K = 2
LN_EPS = 1e-5
RMS_EPS = 1e-6
LANES = 128
VMEM_LIMIT = 56 * 1024 * 1024


def _tile(n, want, align=LANES):
    if n <= want:
        return n
    t = want - want % align
    while n % t:
        t -= align
    return t


def _params(*sem):
    return pltpu.CompilerParams(dimension_semantics=sem, vmem_limit_bytes=VMEM_LIMIT)


def _mod_kernel(c_ref, w_ref, b_ref, o_ref):
    c = c_ref[...]
    cond = (c * jax.nn.sigmoid(c)).astype(BF16)
    o_ref[...] = jnp.dot(cond, w_ref[...].astype(BF16), preferred_element_type=F32) + b_ref[...]


def mod_table(c, mod_w, mod_b):
    depth, d, n = mod_w.shape
    b = c.shape[0]
    tn = _tile(n, 1024)
    out = pl.pallas_call(
        _mod_kernel,
        grid=(depth, n // tn),
        in_specs=[
            pl.BlockSpec((b, d), lambda i, j: (0, 0)),
            pl.BlockSpec((None, d, tn), lambda i, j: (i, 0, j)),
            pl.BlockSpec((None, 1, tn), lambda i, j: (i, 0, j)),
        ],
        out_specs=pl.BlockSpec((None, b, tn), lambda i, j: (i, 0, j)),
        out_shape=jax.ShapeDtypeStruct((depth, b, n), F32),
        compiler_params=_params("parallel", "parallel"),
        name="mod_table",
    )(c, mod_w, mod_b.reshape(depth, 1, n))
    return out.reshape(depth * b * N_MOD, 1, d)


def _mod_spec(d, layer, nb, tiles_per_batch, chunk):
    def index(m, *_):
        return ((layer * nb + m // tiles_per_batch) * N_MOD + chunk, 0, 0)
    return pl.BlockSpec((1, 1, d), index)


def _rope_table_kernel(pos_ref, invf_ref, cos_ref, sin_ref):
    ang = pos_ref[...].astype(F32) * invf_ref[...]
    lane = lax.broadcasted_iota(jnp.int32, ang.shape, 1)
    live = lane < ROPE_DIM
    cos_ref[...] = jnp.where(live, jnp.cos(ang), 0.0)
    sin_ref[...] = jnp.where(live, jnp.sin(ang), 0.0)


def rope_tables(positions):
    t = positions.size
    tm = _tile(t, 1024)
    inv_freq = ROPE_THETA ** (-jnp.arange(ROPE_HALF, dtype=F32) / ROPE_HALF)
    invf = jnp.concatenate([inv_freq, inv_freq, jnp.zeros((LANES - ROPE_DIM,), F32)]).reshape(1, LANES)
    return pl.pallas_call(
        _rope_table_kernel,
        grid=(t // tm,),
        in_specs=[pl.BlockSpec((tm, 1), lambda m: (m, 0)), pl.BlockSpec((1, LANES), lambda m: (0, 0))],
        out_specs=[pl.BlockSpec((tm, LANES), lambda m: (m, 0))] * 2,
        out_shape=[jax.ShapeDtypeStruct((t, LANES), F32)] * 2,
        compiler_params=_params("parallel"),
        name="rope_tables",
    )(positions.reshape(t, 1), invf)


def _rope(x, cos, sin):
    lane = lax.broadcasted_iota(jnp.int32, x.shape, 1)
    rot = jnp.where(lane < ROPE_HALF, -pltpu.roll(x, LANES - ROPE_HALF, 1), pltpu.roll(x, ROPE_HALF, 1))
    return x * cos + rot * sin


def _modulate_into(h_ref, x_ref, sh_ref, sc_ref):
    @pl.when(pl.program_id(1) == 0)
    def _():
        h_ref[...] = (x_ref[...] * (1.0 + sc_ref[0]) + sh_ref[0]).astype(h_ref.dtype)


def _modmm_kernel(x_ref, sh_ref, sc_ref, w_ref, cs_ref, o_ref, h_ref):
    _modulate_into(h_ref, x_ref, sh_ref, sc_ref)
    acc = jnp.dot(h_ref[...], w_ref[...], preferred_element_type=F32)
    o_ref[...] = (acc * cs_ref[...]).astype(o_ref.dtype)


def modmm(x, modr, layer, nb, shift_chunk, w, col_scale, tm=1024, tn=1024):
    t, d = x.shape
    n = w.shape[1]
    tm, tn = _tile(t // nb, tm), _tile(n, tn)
    tpb = (t // nb) // tm
    return pl.pallas_call(
        _modmm_kernel,
        grid=(t // tm, n // tn),
        in_specs=[
            pl.BlockSpec((tm, d), lambda m, j: (m, 0)),
            _mod_spec(d, layer, nb, tpb, shift_chunk),
            _mod_spec(d, layer, nb, tpb, shift_chunk + 1),
            pl.BlockSpec((d, tn), lambda m, j: (0, j)),
            pl.BlockSpec((1, tn), lambda m, j: (0, j)),
        ],
        out_specs=pl.BlockSpec((tm, tn), lambda m, j: (m, j)),
        out_shape=jax.ShapeDtypeStruct((t, n), BF16),
        scratch_shapes=[pltpu.VMEM((tm, d), BF16)],
        compiler_params=_params("parallel", "arbitrary"),
        name="modmm",
    )(x, modr, modr, w, col_scale)


def _modmm_swiglu_kernel(x_ref, sh_ref, sc_ref, wg_ref, wu_ref, o_ref, h_ref):
    _modulate_into(h_ref, x_ref, sh_ref, sc_ref)
    h = h_ref[...]
    g = jnp.dot(h, wg_ref[...], preferred_element_type=F32)
    u = jnp.dot(h, wu_ref[...], preferred_element_type=F32)
    o_ref[...] = (g * jax.nn.sigmoid(g) * u).astype(o_ref.dtype)


def modmm_swiglu(x, modr, layer, nb, shift_chunk, w_gu, tm=1024, tn=512):
    t, d = x.shape
    f = w_gu.shape[1] // 2
    tm, tn = _tile(t // nb, tm), _tile(f, tn)
    tpb = (t // nb) // tm
    nf = f // tn
    return pl.pallas_call(
        _modmm_swiglu_kernel,
        grid=(t // tm, nf),
        in_specs=[
            pl.BlockSpec((tm, d), lambda m, j: (m, 0)),
            _mod_spec(d, layer, nb, tpb, shift_chunk),
            _mod_spec(d, layer, nb, tpb, shift_chunk + 1),
            pl.BlockSpec((d, tn), lambda m, j: (0, j)),
            pl.BlockSpec((d, tn), lambda m, j: (0, j + nf)),
        ],
        out_specs=pl.BlockSpec((tm, tn), lambda m, j: (m, j)),
        out_shape=jax.ShapeDtypeStruct((t, f), BF16),
        scratch_shapes=[pltpu.VMEM((tm, d), BF16)],
        compiler_params=_params("parallel", "arbitrary"),
        name="modmm_swiglu",
    )(x, modr, modr, w_gu, w_gu)


def _res_ln(x, y, gate, ln_g, ln_b, alpha):
    v = alpha * x + (1.0 + gate) * y
    mu = jnp.mean(v, axis=-1, keepdims=True)
    dv = v - mu
    var = jnp.mean(dv * dv, axis=-1, keepdims=True)
    return dv * lax.rsqrt(var + LN_EPS) * ln_g + ln_b


def _mm_res_ln_kernel(a_ref, w_ref, x_ref, gate_ref, lng_ref, lnb_ref, o_ref, acc_ref, *, alpha, nk):
    k = pl.program_id(1)
    part = jnp.dot(a_ref[...], w_ref[...], preferred_element_type=F32)

    @pl.when(k == 0)
    def _():
        acc_ref[...] = part

    @pl.when(k > 0)
    def _():
        acc_ref[...] += part

    @pl.when(k == nk - 1)
    def _():
        o_ref[...] = _res_ln(x_ref[...], acc_ref[...], gate_ref[0], lng_ref[...], lnb_ref[...], alpha)


def mm_res_ln(a, w, x, modr, layer, nb, gate_chunk, ln_g, ln_b, alpha, tm=512, tk=2048):
    t, kdim = a.shape
    d = w.shape[1]
    tm, tk = _tile(t // nb, tm), _tile(kdim, tk)
    tpb = (t // nb) // tm
    nk = kdim // tk
    return pl.pallas_call(
        functools.partial(_mm_res_ln_kernel, alpha=alpha, nk=nk),
        grid=(t // tm, nk),
        in_specs=[
            pl.BlockSpec((tm, tk), lambda m, k: (m, k)),
            pl.BlockSpec((tk, d), lambda m, k: (k, 0)),
            pl.BlockSpec((tm, d), lambda m, k: (m, 0)),
            _mod_spec(d, layer, nb, tpb, gate_chunk),
            pl.BlockSpec((1, d), lambda m, k: (0, 0)),
            pl.BlockSpec((1, d), lambda m, k: (0, 0)),
        ],
        out_specs=pl.BlockSpec((tm, d), lambda m, k: (m, 0)),
        out_shape=jax.ShapeDtypeStruct((t, d), F32),
        scratch_shapes=[pltpu.VMEM((tm, d), F32)],
        compiler_params=_params("parallel", "arbitrary"),
        name="mm_res_ln",
    )(a, w, x, modr, ln_g.reshape(1, d), ln_b.reshape(1, d))


def _sb_attn_kernel(q_ref, k_ref, v_ref, o_ref, *, tq, nq):
    row = lax.broadcasted_iota(jnp.int32, (tq, tq), 0)
    col = lax.broadcasted_iota(jnp.int32, (tq, tq), 1)
    strict = col < row
    later = strict.astype(BF16)

    def block(q, kb, vb, carry, masked):
        run, acc = carry
        z = lax.dot_general(q, kb, (((1,), (1,)), ((), ())), preferred_element_type=F32)
        sp = jnp.maximum(z, 0.0) + jnp.log1p(jnp.exp(-jnp.abs(z)))
        lomb = jnp.where(strict, -sp, 0.0) if masked else -sp
        hi = lomb.astype(BF16)
        lo = (lomb - hi.astype(F32)).astype(BF16)
        tail = jnp.dot(hi, later, preferred_element_type=F32) + jnp.dot(lo, later, preferred_element_type=F32)
        a = jnp.exp(z - sp + tail + run)
        if masked:
            a = jnp.where(strict, a, 0.0)
        acc = acc + jnp.dot(a.astype(BF16), vb, preferred_element_type=F32)
        run = run + jnp.sum(lomb, axis=-1, keepdims=True)
        return run, acc

    def qbody(qi, _):
        q0 = pl.multiple_of(qi * tq, tq)
        q = q_ref[pl.ds(q0, tq), :]
        carry = (jnp.zeros((tq, 1), F32), jnp.zeros((tq, HEAD_DIM), F32))
        carry = block(q, k_ref[pl.ds(q0, tq), :], v_ref[pl.ds(q0, tq), :], carry, True)

        def kbody(i, carry):
            k0 = pl.multiple_of((qi - 1 - i) * tq, tq)
            return block(q, k_ref[pl.ds(k0, tq), :], v_ref[pl.ds(k0, tq), :], carry, False)

        _, acc = lax.fori_loop(0, qi, kbody, carry)
        o_ref[pl.ds(q0, tq), :] = acc.astype(o_ref.dtype)
        return 0

    lax.fori_loop(0, nq, qbody, 0)


def sb_attention(qkv, nb, tq=256):
    t, n3 = qkv.shape
    s = t // nb
    nh = n3 // (3 * HEAD_DIM)
    tq = _tile(s, tq)
    return pl.pallas_call(
        functools.partial(_sb_attn_kernel, tq=tq, nq=s // tq),
        grid=(nb, nh),
        in_specs=[
            pl.BlockSpec((s, HEAD_DIM), lambda b, h: (b, h)),
            pl.BlockSpec((s, HEAD_DIM), lambda b, h: (b, nh + h)),
            pl.BlockSpec((s, HEAD_DIM), lambda b, h: (b, 2 * nh + h)),
        ],
        out_specs=pl.BlockSpec((s, HEAD_DIM), lambda b, h: (b, h)),
        out_shape=jax.ShapeDtypeStruct((t, nh * HEAD_DIM), BF16),
        compiler_params=_params("parallel", "parallel"),
        name="sb_attention",
    )(qkv, qkv, qkv)


def _mla_in_kernel(x_ref, sh_ref, sc_ref, w_ref, qn_ref, kvn_ref, cos_ref, sin_ref,
                   cq_ref, ckv_ref, kr_ref, *, q_rank, kv_rank):
    h = (x_ref[...] * (1.0 + sc_ref[0]) + sh_ref[0]).astype(BF16)
    lat = jnp.dot(h, w_ref[...], preferred_element_type=F32)

    def rms(v, g):
        return v * lax.rsqrt(jnp.mean(v * v, axis=-1, keepdims=True) + RMS_EPS) * g

    cq_ref[...] = rms(lat[:, :q_rank], qn_ref[...]).astype(cq_ref.dtype)
    ckv_ref[...] = rms(lat[:, q_rank:q_rank + kv_rank], kvn_ref[...]).astype(ckv_ref.dtype)
    kr_ref[...] = _rope(lat[:, q_rank + kv_rank:], cos_ref[...], sin_ref[...]).astype(kr_ref.dtype)


def mla_in(x, modr, layer, nb, w_in_pad, q_norm, kv_norm, cos, sin, tm=512):
    t, d = x.shape
    q_rank, kv_rank = q_norm.shape[0], kv_norm.shape[0]
    n = w_in_pad.shape[1]
    assert n == q_rank + kv_rank + LANES
    tm = _tile(t // nb, tm)
    tpb = (t // nb) // tm
    return pl.pallas_call(
        functools.partial(_mla_in_kernel, q_rank=q_rank, kv_rank=kv_rank),
        grid=(t // tm,),
        in_specs=[
            pl.BlockSpec((tm, d), lambda m: (m, 0)),
            _mod_spec(d, layer, nb, tpb, 0),
            _mod_spec(d, layer, nb, tpb, 1),
            pl.BlockSpec((d, n), lambda m: (0, 0)),
            pl.BlockSpec((1, q_rank), lambda m: (0, 0)),
            pl.BlockSpec((1, kv_rank), lambda m: (0, 0)),
            pl.BlockSpec((tm, LANES), lambda m: (m, 0)),
            pl.BlockSpec((tm, LANES), lambda m: (m, 0)),
        ],
        out_specs=[
            pl.BlockSpec((tm, q_rank), lambda m: (m, 0)),
            pl.BlockSpec((tm, kv_rank), lambda m: (m, 0)),
            pl.BlockSpec((tm, LANES), lambda m: (m, 0)),
        ],
        out_shape=[
            jax.ShapeDtypeStruct((t, q_rank), BF16),
            jax.ShapeDtypeStruct((t, kv_rank), BF16),
            jax.ShapeDtypeStruct((t, LANES), BF16),
        ],
        compiler_params=_params("parallel"),
        name="mla_in",
    )(x, modr, modr, w_in_pad, q_norm.reshape(1, q_rank), kv_norm.reshape(1, kv_rank), cos, sin)


def _mm_kernel(a_ref, w_ref, o_ref):
    o_ref[...] = jnp.dot(a_ref[...], w_ref[...], preferred_element_type=F32).astype(o_ref.dtype)


def mm(a, w, tm=1024, tn=1024):
    t, k = a.shape
    n = w.shape[1]
    tm, tn = _tile(t, tm), _tile(n, tn)
    return pl.pallas_call(
        _mm_kernel,
        grid=(t // tm, n // tn),
        in_specs=[pl.BlockSpec((tm, k), lambda m, j: (m, 0)), pl.BlockSpec((k, tn), lambda m, j: (0, j))],
        out_specs=pl.BlockSpec((tm, tn), lambda m, j: (m, j)),
        out_shape=jax.ShapeDtypeStruct((t, n), BF16),
        compiler_params=_params("parallel", "parallel"),
        name="mm",
    )(a, w)


def _mm_q_rope_kernel(a_ref, w_ref, cos_ref, sin_ref, o_ref, *, scale, heads):
    acc = jnp.dot(a_ref[...], w_ref[...], preferred_element_type=F32)
    cos, sin = cos_ref[...], sin_ref[...]
    for h in range(heads):
        c0 = 2 * LANES * h
        o_ref[:, c0:c0 + LANES] = (acc[:, c0:c0 + LANES] * scale).astype(o_ref.dtype)
        roped = _rope(acc[:, c0 + LANES:c0 + 2 * LANES], cos, sin)
        o_ref[:, c0 + LANES:c0 + 2 * LANES] = (roped * scale).astype(o_ref.dtype)


def mm_q_rope(a, w, cos, sin, scale, tm=1024, heads_per_tile=4):
    t, k = a.shape
    n = w.shape[1]
    tm = _tile(t, tm)
    tn = _tile(n, 2 * LANES * heads_per_tile)
    return pl.pallas_call(
        functools.partial(_mm_q_rope_kernel, scale=scale, heads=tn // (2 * LANES)),
        grid=(t // tm, n // tn),
        in_specs=[
            pl.BlockSpec((tm, k), lambda m, j: (m, 0)),
            pl.BlockSpec((k, tn), lambda m, j: (0, j)),
            pl.BlockSpec((tm, LANES), lambda m, j: (m, 0)),
            pl.BlockSpec((tm, LANES), lambda m, j: (m, 0)),
        ],
        out_specs=pl.BlockSpec((tm, tn), lambda m, j: (m, j)),
        out_shape=jax.ShapeDtypeStruct((t, n), BF16),
        compiler_params=_params("parallel", "parallel"),
        name="mm_q_rope",
    )(a, w, cos, sin)


def _mla_attn_kernel(q_ref, kv_ref, kr_ref, o_ref, kcat_ref, *, tq, nq):
    kcat_ref[:, :HEAD_DIM] = kv_ref[:, :HEAD_DIM]
    kcat_ref[:, HEAD_DIM:] = kr_ref[...]
    row = lax.broadcasted_iota(jnp.int32, (tq, tq), 0)
    col = lax.broadcasted_iota(jnp.int32, (tq, tq), 1)
    causal = col <= row

    def block(q, k0, carry, masked):
        m, l, acc = carry
        kb = kcat_ref[pl.ds(k0, tq), :]
        vb = kv_ref[pl.ds(k0, tq), HEAD_DIM:]
        s = lax.dot_general(q, kb, (((1,), (1,)), ((), ())), preferred_element_type=F32)
        if masked:
            s = jnp.where(causal, s, -jnp.inf)
        m_new = jnp.maximum(m, jnp.max(s, axis=-1, keepdims=True))
        p = jnp.exp(s - m_new)
        corr = jnp.exp(m - m_new)
        l = l * corr + jnp.sum(p, axis=-1, keepdims=True)
        acc = acc * corr + jnp.dot(p.astype(BF16), vb, preferred_element_type=F32)
        return m_new, l, acc

    def qbody(qi, _):
        q0 = pl.multiple_of(qi * tq, tq)
        q = q_ref[pl.ds(q0, tq), :]
        carry = (jnp.full((tq, 1), -jnp.inf, F32), jnp.zeros((tq, 1), F32), jnp.zeros((tq, HEAD_DIM), F32))
        carry = block(q, q0, carry, True)

        def kbody(i, carry):
            return block(q, pl.multiple_of(i * tq, tq), carry, False)

        _, l, acc = lax.fori_loop(0, qi, kbody, carry)
        o_ref[pl.ds(q0, tq), :] = (acc / l).astype(o_ref.dtype)
        return 0

    lax.fori_loop(0, nq, qbody, 0)


def mla_attention(q, kv, kr, nb, tq=256):
    t = q.shape[0]
    s = t // nb
    nh = q.shape[1] // (2 * HEAD_DIM)
    tq = _tile(s, tq)
    return pl.pallas_call(
        functools.partial(_mla_attn_kernel, tq=tq, nq=s // tq),
        grid=(nb, nh),
        in_specs=[
            pl.BlockSpec((s, 2 * HEAD_DIM), lambda b, h: (b, h)),
            pl.BlockSpec((s, 2 * HEAD_DIM), lambda b, h: (b, h)),
            pl.BlockSpec((s, LANES), lambda b, h: (b, 0)),
        ],
        out_specs=pl.BlockSpec((s, HEAD_DIM), lambda b, h: (b, h)),
        out_shape=jax.ShapeDtypeStruct((t, nh * HEAD_DIM), BF16),
        scratch_shapes=[pltpu.VMEM((s, 2 * HEAD_DIM), BF16)],
        compiler_params=_params("parallel", "parallel"),
        name="mla_attention",
    )(q, kv, kr)


def _router_kernel(x_ref, sh_ref, sc_ref, wr_ref, h_ref, idx_ref, gate_ref, cnt_ref, tri_ref, base_ref,
                   *, tm, n_exp):
    @pl.when(pl.program_id(0) == 0)
    def _():
        r = lax.broadcasted_iota(jnp.int32, (tm, tm), 0)
        c = lax.broadcasted_iota(jnp.int32, (tm, tm), 1)
        tri_ref[...] = (c <= r).astype(BF16)
        base_ref[...] = jnp.zeros_like(base_ref)

    h = x_ref[...] * (1.0 + sc_ref[0]) + sh_ref[0]
    h_hi = h.astype(BF16)
    h_ref[...] = h_hi
    h_lo = (h - h_hi.astype(F32)).astype(BF16)
    w = wr_ref[...]
    w_hi = w.astype(BF16)
    w_lo = (w - w_hi.astype(F32)).astype(BF16)
    logits = (jnp.dot(h_hi, w_hi, preferred_element_type=F32)
              + jnp.dot(h_lo, w_hi, preferred_element_type=F32)
              + jnp.dot(h_hi, w_lo, preferred_element_type=F32))
    lane = lax.broadcasted_iota(jnp.int32, logits.shape, 1).astype(F32)
    lg = jnp.where(lane < n_exp, logits, -jnp.inf)
    m1 = jnp.max(lg, axis=-1, keepdims=True)
    i1 = jnp.min(jnp.where(lg == m1, lane, float(LANES)), axis=-1, keepdims=True)
    lg2 = jnp.where(lane == i1, -jnp.inf, lg)
    m2 = jnp.max(lg2, axis=-1, keepdims=True)
    i2 = jnp.min(jnp.where(lg2 == m2, lane, float(LANES)), axis=-1, keepdims=True)
    e = jnp.exp(m2 - m1)
    g1 = 1.0 / (1.0 + e)
    g2 = e * g1
    oh1 = lane == i1
    oh2 = lane == i2
    onehot = jnp.where(oh1 | oh2, 1.0, 0.0).astype(BF16)
    tot = base_ref[...] + jnp.dot(tri_ref[...], onehot, preferred_element_type=F32)
    r1 = jnp.sum(jnp.where(oh1, tot, 0.0), axis=-1, keepdims=True) - 1.0
    r2 = jnp.sum(jnp.where(oh2, tot, 0.0), axis=-1, keepdims=True) - 1.0
    last = tot[tm - 1:tm, :]
    base_ref[...] = last
    cnt_ref[...] = jnp.broadcast_to(last, cnt_ref.shape)
    info = jnp.where(lane == 0, i1, jnp.where(lane == 1, i2, jnp.where(lane == 2, r1, jnp.where(lane == 3, r2, 0.0))))
    idx_ref[...] = info.astype(jnp.int32)
    gate_ref[...] = jnp.where(lane == 0, g1, jnp.where(lane == 1, g2, 0.0))


def router(x, modr, layer, nb, w_router, tm=512):
    t, d = x.shape
    n_exp = w_router.shape[1]
    tm = _tile(t // nb, tm)
    tpb = (t // nb) // tm
    wr = jnp.pad(w_router, ((0, 0), (0, LANES - n_exp)))
    return pl.pallas_call(
        functools.partial(_router_kernel, tm=tm, n_exp=n_exp),
        grid=(t // tm,),
        in_specs=[
            pl.BlockSpec((tm, d), lambda m: (m, 0)),
            _mod_spec(d, layer, nb, tpb, 3),
            _mod_spec(d, layer, nb, tpb, 4),
            pl.BlockSpec((d, LANES), lambda m: (0, 0)),
        ],
        out_specs=[
            pl.BlockSpec((tm, d), lambda m: (m, 0)),
            pl.BlockSpec((tm, LANES), lambda m: (m, 0)),
            pl.BlockSpec((tm, LANES), lambda m: (m, 0)),
            pl.BlockSpec((8, LANES), lambda m: (0, 0)),
        ],
        out_shape=[
            jax.ShapeDtypeStruct((t, d), BF16),
            jax.ShapeDtypeStruct((t, LANES), jnp.int32),
            jax.ShapeDtypeStruct((t, LANES), F32),
            jax.ShapeDtypeStruct((8, LANES), F32),
        ],
        scratch_shapes=[pltpu.VMEM((tm, tm), BF16), pltpu.VMEM((1, LANES), F32)],
        compiler_params=_params("arbitrary"),
        name="router",
    )(x, modr, modr, wr)


def _gather_rows_kernel(idx_ref, src_ref, o_ref, sem, *, tg):
    base = pl.program_id(0) * tg

    def row_copy(r):
        return pltpu.make_async_copy(src_ref.at[idx_ref[0, 0, r]], o_ref.at[base + r], sem)

    def start(r, _):
        row_copy(r).start()
        return 0

    def wait(r, _):
        row_copy(r).wait()
        return 0

    lax.fori_loop(0, tg, start, 0)
    lax.fori_loop(0, tg, wait, 0)


def gather_rows(src, idx, tg=512):
    n = idx.shape[0]
    d = src.shape[1]
    tg = _tile(n, tg)
    out = pl.pallas_call(
        functools.partial(_gather_rows_kernel, tg=tg),
        grid=(n // tg,),
        in_specs=[
            pl.BlockSpec((1, 1, tg), lambda m: (m, 0, 0), memory_space=pltpu.SMEM),
            pl.BlockSpec(memory_space=pl.ANY),
        ],
        out_specs=pl.BlockSpec(memory_space=pl.ANY),
        out_shape=jax.ShapeDtypeStruct((n, d // LANES, LANES), src.dtype),
        scratch_shapes=[pltpu.SemaphoreType.DMA(())],
        compiler_params=_params("arbitrary"),
        name="gather_rows",
    )(idx.reshape(n // tg, 1, tg), src.reshape(src.shape[0], d // LANES, LANES))
    return out.reshape(n, d)


def _gmm_swiglu_kernel(te_ref, nu_ref, a_ref, wg_ref, wu_ref, o_ref):
    used = pl.program_id(1) < nu_ref[0]

    @pl.when(used)
    def _():
        a = a_ref[...]
        g = jnp.dot(a, wg_ref[...], preferred_element_type=F32)
        u = jnp.dot(a, wu_ref[...], preferred_element_type=F32)
        o_ref[...] = (g * jax.nn.sigmoid(g) * u).astype(o_ref.dtype)

    @pl.when(jnp.logical_not(used))
    def _():
        o_ref[...] = jnp.zeros_like(o_ref)


def gmm_swiglu(a, w_gu, tile_expert, n_used, tm, tn=512):
    ms, d = a.shape
    f = w_gu.shape[2] // 2
    tn = _tile(f, tn)
    nf = f // tn
    return pl.pallas_call(
        _gmm_swiglu_kernel,
        grid_spec=pltpu.PrefetchScalarGridSpec(
            num_scalar_prefetch=2,
            grid=(nf, ms // tm),
            in_specs=[
                pl.BlockSpec((tm, d), lambda j, m, te, nu: (m, 0)),
                pl.BlockSpec((None, d, tn), lambda j, m, te, nu: (te[m], 0, j)),
                pl.BlockSpec((None, d, tn), lambda j, m, te, nu: (te[m], 0, j + nf)),
            ],
            out_specs=pl.BlockSpec((tm, tn), lambda j, m, te, nu: (m, j)),
        ),
        out_shape=jax.ShapeDtypeStruct((ms, f), BF16),
        compiler_params=_params("parallel", "arbitrary"),
        name="gmm_swiglu",
    )(tile_expert, n_used, a, w_gu, w_gu)


def _gmm_kernel(te_ref, nu_ref, a_ref, w_ref, o_ref):
    used = pl.program_id(1) < nu_ref[0]

    @pl.when(used)
    def _():
        o_ref[...] = jnp.dot(a_ref[...], w_ref[...], preferred_element_type=F32).astype(o_ref.dtype)

    @pl.when(jnp.logical_not(used))
    def _():
        o_ref[...] = jnp.zeros_like(o_ref)


def gmm(a, w, tile_expert, n_used, tm, tn=512):
    ms, k = a.shape
    n = w.shape[2]
    tn = _tile(n, tn)
    return pl.pallas_call(
        _gmm_kernel,
        grid_spec=pltpu.PrefetchScalarGridSpec(
            num_scalar_prefetch=2,
            grid=(n // tn, ms // tm),
            in_specs=[
                pl.BlockSpec((tm, k), lambda j, m, te, nu: (m, 0)),
                pl.BlockSpec((None, k, tn), lambda j, m, te, nu: (te[m], 0, j)),
            ],
            out_specs=pl.BlockSpec((tm, tn), lambda j, m, te, nu: (m, j)),
        ),
        out_shape=jax.ShapeDtypeStruct((ms, n), F32),
        compiler_params=_params("parallel", "arbitrary"),
        name="gmm",
    )(tile_expert, n_used, a, w)


def _combine_ln_kernel(p0_ref, p1_ref, y_ref, g0_ref, g1_ref, x_ref, gmod_ref, lng_ref, lnb_ref, o_ref,
                       y0_ref, y1_ref, sem, *, tm, alpha):
    def copies(r):
        return (pltpu.make_async_copy(y_ref.at[p0_ref[0, 0, r]], y0_ref.at[r], sem.at[0]),
                pltpu.make_async_copy(y_ref.at[p1_ref[0, 0, r]], y1_ref.at[r], sem.at[1]))

    def start(r, _):
        for cp in copies(r):
            cp.start()
        return 0

    def wait(r, _):
        for cp in copies(r):
            cp.wait()
        return 0

    lax.fori_loop(0, tm, start, 0)
    lax.fori_loop(0, tm, wait, 0)
    y = g0_ref[...] * y0_ref[...] + g1_ref[...] * y1_ref[...]
    v = alpha * x_ref[...] + (1.0 + gmod_ref[0]) * y
    mu = jnp.mean(v, axis=(1, 2), keepdims=True)
    dv = v - mu
    var = jnp.mean(dv * dv, axis=(1, 2), keepdims=True)
    o_ref[...] = dv * lax.rsqrt(var + LN_EPS) * lng_ref[...] + lnb_ref[...]


def combine_ln(y_sorted, pos0, pos1, gates, x, modr, layer, nb, ln_g, ln_b, alpha, tm=256):
    t, d = x.shape
    tm = _tile(t // nb, tm)
    tpb = (t // nb) // tm
    nt = t // tm
    ns = d // LANES

    def slab(a):
        return a.reshape(a.shape[0], ns, LANES)

    row_spec = pl.BlockSpec((tm, ns, LANES), lambda m: (m, 0, 0))
    gate_spec = pl.BlockSpec((tm, 1, 1), lambda m: (m, 0, 0))
    vec_spec = pl.BlockSpec((ns, LANES), lambda m: (0, 0))
    out = pl.pallas_call(
        functools.partial(_combine_ln_kernel, tm=tm, alpha=alpha),
        grid=(nt,),
        in_specs=[
            pl.BlockSpec((1, 1, tm), lambda m: (m, 0, 0), memory_space=pltpu.SMEM),
            pl.BlockSpec((1, 1, tm), lambda m: (m, 0, 0), memory_space=pltpu.SMEM),
            pl.BlockSpec(memory_space=pl.ANY),
            gate_spec,
            gate_spec,
            row_spec,
            pl.BlockSpec((1, ns, LANES), _mod_spec(d, layer, nb, tpb, 5).index_map),
            vec_spec,
            vec_spec,
        ],
        out_specs=row_spec,
        out_shape=jax.ShapeDtypeStruct((t, ns, LANES), F32),
        scratch_shapes=[pltpu.VMEM((tm, ns, LANES), y_sorted.dtype), pltpu.VMEM((tm, ns, LANES), y_sorted.dtype),
                        pltpu.SemaphoreType.DMA((2,))],
        compiler_params=_params("arbitrary"),
        name="combine_ln",
    )(pos0.reshape(nt, 1, tm), pos1.reshape(nt, 1, tm), slab(y_sorted), gates[:, 0].reshape(t, 1, 1),
      gates[:, 1].reshape(t, 1, 1), slab(x), slab(modr.reshape(-1, d)), ln_g.reshape(ns, LANES),
      ln_b.reshape(ns, LANES))
    return out.reshape(t, d)


def moe_layer(x, modr, layer, nb, w_router, w_gu, w_down, ln_g, ln_b, alpha, tm_g=512):
    t, d = x.shape
    n_exp = w_router.shape[1]
    h, info, gates, cnt = router(x, modr, layer, nb, w_router)
    counts = cnt[0, :n_exp].astype(jnp.int32)
    sizes = (counts + tm_g - 1) // tm_g * tm_g
    ends = jnp.cumsum(sizes)
    starts = ends - sizes
    pos0 = starts[info[:, 0]] + info[:, 2]
    pos1 = starts[info[:, 1]] + info[:, 3]
    ms = t * TOP_K + n_exp * tm_g
    tok = jnp.arange(t, dtype=jnp.int32)
    slot_token = jnp.zeros((ms,), jnp.int32).at[pos0].set(tok).at[pos1].set(tok)
    tile_start = jnp.arange(ms // tm_g, dtype=jnp.int32) * tm_g
    tile_expert = jnp.minimum(jnp.sum(tile_start[:, None] >= ends[None, :], axis=1), n_exp - 1).astype(jnp.int32)
    n_used = (ends[-1:] // tm_g).astype(jnp.int32)

    h_sorted = gather_rows(h, slot_token)
    act = gmm_swiglu(h_sorted, w_gu, tile_expert, n_used, tm_g)
    y_sorted = gmm(act, w_down, tile_expert, n_used, tm_g)
    return combine_ln(y_sorted, pos0, pos1, gates, x, modr, layer, nb, ln_g, ln_b, alpha)


def kernel(x, c, positions, mod_w, mod_b, ln_g, ln_b, sb_w_qkv, sb_w_o, mla_w_in, mla_q_norm, mla_kv_norm,
           mla_w_uq, mla_w_ukv, mla_w_o, ffn_w_gu, ffn_w_down, moe_router, moe_w_gu, moe_w_down):
    nb, s, d = x.shape
    depth = mod_w.shape[0]
    t = nb * s
    alpha = float((2 * depth) ** 0.25)
    xf = x.reshape(t, d)

    modr = mod_table(c, mod_w, mod_b)
    cos, sin = rope_tables(positions)

    sb_heads = sb_w_o.shape[1] // HEAD_DIM
    sb_scale = HEAD_DIM ** -0.5
    sb_col_scale = jnp.concatenate([jnp.full((sb_heads * HEAD_DIM,), sb_scale, F32),
                                    jnp.ones((2 * sb_heads * HEAD_DIM,), F32)]).reshape(1, -1)

    mla_heads = mla_w_o.shape[1] // HEAD_DIM
    q_rank = mla_q_norm.shape[1]
    mla_scale = float((HEAD_DIM + ROPE_DIM) ** -0.5)

    for i in range(depth):
        j = i // 2
        if i % 2 == 0:
            qkv = modmm(xf, modr, i, nb, 0, sb_w_qkv[j].astype(BF16), sb_col_scale)
            o = sb_attention(qkv, nb)
            w_o = sb_w_o[j]
        else:
            w_in = jnp.pad(mla_w_in[j], ((0, 0), (0, LANES - ROPE_DIM))).astype(BF16)
            cq, ckv, kr = mla_in(xf, modr, i, nb, w_in, mla_q_norm[j], mla_kv_norm[j], cos, sin)
            w_uq = mla_w_uq[j].reshape(q_rank, mla_heads, HEAD_DIM + ROPE_DIM)
            w_uq = jnp.pad(w_uq, ((0, 0), (0, 0), (0, LANES - ROPE_DIM))).reshape(q_rank, mla_heads * 2 * LANES)
            q = mm_q_rope(cq, w_uq.astype(BF16), cos, sin, mla_scale)
            kv = mm(ckv, mla_w_ukv[j].astype(BF16))
            o = mla_attention(q, kv, kr, nb)
            w_o = mla_w_o[j]
        xf = mm_res_ln(o, w_o.astype(BF16), xf, modr, i, nb, 2, ln_g[i, 0], ln_b[i, 0], alpha)
        if i % 2 == 0:
            act = modmm_swiglu(xf, modr, i, nb, 3, ffn_w_gu[j].astype(BF16))
            xf = mm_res_ln(act, ffn_w_down[j].astype(BF16), xf, modr, i, nb, 5, ln_g[i, 1], ln_b[i, 1], alpha)
        else:
            xf = moe_layer(xf, modr, i, nb, moe_router[j], moe_w_gu[j].astype(BF16), moe_w_down[j].astype(BF16),
                           ln_g[i, 1], ln_b[i, 1], alpha)
    return xf.reshape(nb, s, d)
```

```python
import functools

import jax
import jax.numpy as jnp
from jax import lax
from jax.experimental import pallas as pl
from jax.experimental.pallas import tpu as pltpu

F32 = jnp.float32
BF16 = jnp.bfloat16

HEAD_DIM = 128
ROPE_DIM = 64
ROPE_HALF = ROPE_DIM // 2
ROPE_THETA = 10000.0
N_MOD = 6
TOP_K = 2
LN_EPS = 1e-5
RMS_EPS = 1e-6
LANES = 128
VMEM_LIMIT = 56 * 1024 * 1024


def _tile(n, want, align=LANES):
    if n <= want:
        return n
    t = want - want % align
    while n % t:
        t -= align
    return t


def _params(*sem):
    return pltpu.CompilerParams(dimension_semantics=sem, vmem_limit_bytes=VMEM_LIMIT)


def _mod_kernel(c_ref, w_ref, b_ref, o_ref):
    c = c_ref[...]
    cond = (c * jax.nn.sigmoid(c)).astype(BF16)
    o_ref[...] = jnp.dot(cond, w_ref[...].astype(BF16), preferred_element_type=F32) + b_ref[...]


def mod_table(c, mod_w, mod_b):
    depth, d, n = mod_w.shape
    b = c.shape[0]
    tn = _tile(n, 1024)
    out = pl.pallas_call(
        _mod_kernel,
        grid=(depth, n // tn),
        in_specs=[
            pl.BlockSpec((b, d), lambda i, j: (0, 0)),
            pl.BlockSpec((None, d, tn), lambda i, j: (i, 0, j)),
            pl.BlockSpec((None, 1, tn), lambda i, j: (i, 0, j)),
        ],
        out_specs=pl.BlockSpec((None, b, tn), lambda i, j: (i, 0, j)),
        out_shape=jax.ShapeDtypeStruct((depth, b, n), F32),
        compiler_params=_params("parallel", "parallel"),
        name="mod_table",
    )(c, mod_w, mod_b.reshape(depth, 1, n))
    return out.reshape(depth * b * N_MOD, 1, d)


def _mod_spec(d, layer, nb, tiles_per_batch, chunk):
    def index(m, *_):
        return ((layer * nb + m // tiles_per_batch) * N_MOD + chunk, 0, 0)
    return pl.BlockSpec((1, 1, d), index)


def _rope_table_kernel(pos_ref, invf_ref, cos_ref, sin_ref):
    ang = pos_ref[...].astype(F32) * invf_ref[...]
    lane = lax.broadcasted_iota(jnp.int32, ang.shape, 1)
    live = lane < ROPE_DIM
    cos_ref[...] = jnp.where(live, jnp.cos(ang), 0.0)
    sin_ref[...] = jnp.where(live, jnp.sin(ang), 0.0)


def rope_tables(positions):
    t = positions.size
    tm = _tile(t, 1024)
    inv_freq = ROPE_THETA ** (-jnp.arange(ROPE_HALF, dtype=F32) / ROPE_HALF)
    invf = jnp.concatenate([inv_freq, inv_freq, jnp.zeros((LANES - ROPE_DIM,), F32)]).reshape(1, LANES)
    return pl.pallas_call(
        _rope_table_kernel,
        grid=(t // tm,),
        in_specs=[pl.BlockSpec((tm, 1), lambda m: (m, 0)), pl.BlockSpec((1, LANES), lambda m: (0, 0))],
        out_specs=[pl.BlockSpec((tm, LANES), lambda m: (m, 0))] * 2,
        out_shape=[jax.ShapeDtypeStruct((t, LANES), F32)] * 2,
        compiler_params=_params("parallel"),
        name="rope_tables",
    )(positions.reshape(t, 1), invf)


def _rope(x, cos, sin):
    lane = lax.broadcasted_iota(jnp.int32, x.shape, 1)
    rot = jnp.where(lane < ROPE_HALF, -pltpu.roll(x, LANES - ROPE_HALF, 1), pltpu.roll(x, ROPE_HALF, 1))
    return x * cos + rot * sin


def _modulate_into(h_ref, x_ref, sh_ref, sc_ref):
    @pl.when(pl.program_id(1) == 0)
    def _():
        h_ref[...] = (x_ref[...] * (1.0 + sc_ref[0]) + sh_ref[0]).astype(h_ref.dtype)


def _modmm_kernel(x_ref, sh_ref, sc_ref, w_ref, cs_ref, o_ref, h_ref):
    _modulate_into(h_ref, x_ref, sh_ref, sc_ref)
    acc = jnp.dot(h_ref[...], w_ref[...], preferred_element_type=F32)
    o_ref[...] = (acc * cs_ref[...]).astype(o_ref.dtype)


def modmm(x, modr, layer, nb, shift_chunk, w, col_scale, tm=1024, tn=1024):
    t, d = x.shape
    n = w.shape[1]
    tm, tn = _tile(t // nb, tm), _tile(n, tn)
    tpb = (t // nb) // tm
    return pl.pallas_call(
        _modmm_kernel,
        grid=(t // tm, n // tn),
        in_specs=[
            pl.BlockSpec((tm, d), lambda m, j: (m, 0)),
            _mod_spec(d, layer, nb, tpb, shift_chunk),
            _mod_spec(d, layer, nb, tpb, shift_chunk + 1),
            pl.BlockSpec((d, tn), lambda m, j: (0, j)),
            pl.BlockSpec((1, tn), lambda m, j: (0, j)),
        ],
        out_specs=pl.BlockSpec((tm, tn), lambda m, j: (m, j)),
        out_shape=jax.ShapeDtypeStruct((t, n), BF16),
        scratch_shapes=[pltpu.VMEM((tm, d), BF16)],
        compiler_params=_params("parallel", "arbitrary"),
        name="modmm",
    )(x, modr, modr, w, col_scale)


def _modmm_swiglu_kernel(x_ref, sh_ref, sc_ref, wg_ref, wu_ref, o_ref, h_ref):
    _modulate_into(h_ref, x_ref, sh_ref, sc_ref)
    h = h_ref[...]
    g = jnp.dot(h, wg_ref[...], preferred_element_type=F32)
    u = jnp.dot(h, wu_ref[...], preferred_element_type=F32)
    o_ref[...] = (g * jax.nn.sigmoid(g) * u).astype(o_ref.dtype)


def modmm_swiglu(x, modr, layer, nb, shift_chunk, w_gu, tm=1024, tn=512):
    t, d = x.shape
    f = w_gu.shape[1] // 2
    tm, tn = _tile(t // nb, tm), _tile(f, tn)
    tpb = (t // nb) // tm
    nf = f // tn
    return pl.pallas_call(
        _modmm_swiglu_kernel,
        grid=(t // tm, nf),
        in_specs=[
            pl.BlockSpec((tm, d), lambda m, j: (m, 0)),
            _mod_spec(d, layer, nb, tpb, shift_chunk),
            _mod_spec(d, layer, nb, tpb, shift_chunk + 1),
            pl.BlockSpec((d, tn), lambda m, j: (0, j)),
            pl.BlockSpec((d, tn), lambda m, j: (0, j + nf)),
        ],
        out_specs=pl.BlockSpec((tm, tn), lambda m, j: (m, j)),
        out_shape=jax.ShapeDtypeStruct((t, f), BF16),
        scratch_shapes=[pltpu.VMEM((tm, d), BF16)],
        compiler_params=_params("parallel", "arbitrary"),
        name="modmm_swiglu",
    )(x, modr, modr, w_gu, w_gu)


def _res_ln(x, y, gate, ln_g, ln_b, alpha):
    v = alpha * x + (1.0 + gate) * y
    mu = jnp.mean(v, axis=-1, keepdims=True)
    dv = v - mu
    var = jnp.mean(dv * dv, axis=-1, keepdims=True)
    return dv * lax.rsqrt(var + LN_EPS) * ln_g + ln_b


def _mm_res_ln_kernel(a_ref, w_ref, x_ref, gate_ref, lng_ref, lnb_ref, o_ref, acc_ref, *, alpha, nk):
    k = pl.program_id(1)
    part = jnp.dot(a_ref[...], w_ref[...], preferred_element_type=F32)

    @pl.when(k == 0)
    def _():
        acc_ref[...] = part

    @pl.when(k > 0)
    def _():
        acc_ref[...] += part

    @pl.when(k == nk - 1)
    def _():
        o_ref[...] = _res_ln(x_ref[...], acc_ref[...], gate_ref[0], lng_ref[...], lnb_ref[...], alpha)


def mm_res_ln(a, w, x, modr, layer, nb, gate_chunk, ln_g, ln_b, alpha, tm=512, tk=2048):
    t, kdim = a.shape
    d = w.shape[1]
    tm, tk = _tile(t // nb, tm), _tile(kdim, tk)
    tpb = (t // nb) // tm
    nk = kdim // tk
    return pl.pallas_call(
        functools.partial(_mm_res_ln_kernel, alpha=alpha, nk=nk),
        grid=(t // tm, nk),
        in_specs=[
            pl.BlockSpec((tm, tk), lambda m, k: (m, k)),
            pl.BlockSpec((tk, d), lambda m, k: (k, 0)),
            pl.BlockSpec((tm, d), lambda m, k: (m, 0)),
            _mod_spec(d, layer, nb, tpb, gate_chunk),
            pl.BlockSpec((1, d), lambda m, k: (0, 0)),
            pl.BlockSpec((1, d), lambda m, k: (0, 0)),
        ],
        out_specs=pl.BlockSpec((tm, d), lambda m, k: (m, 0)),
        out_shape=jax.ShapeDtypeStruct((t, d), F32),
        scratch_shapes=[pltpu.VMEM((tm, d), F32)],
        compiler_params=_params("parallel", "arbitrary"),
        name="mm_res_ln",
    )(a, w, x, modr, ln_g.reshape(1, d), ln_b.reshape(1, d))


def _head_cols(h):
    return slice(h * HEAD_DIM, (h + 1) * HEAD_DIM)


def _sb_attn_kernel(q_ref, k_ref, v_ref, o_ref, *, tq, nq, heads):
    row = lax.broadcasted_iota(jnp.int32, (tq, tq), 0)
    col = lax.broadcasted_iota(jnp.int32, (tq, tq), 1)
    strict = col < row
    later = strict.astype(BF16)

    def blocks(qs, k0, carries, masked):
        hs = range(heads)
        zs = [lax.dot_general(qs[h], k_ref[pl.ds(k0, tq), _head_cols(h)], (((1,), (1,)), ((), ())),
                              preferred_element_type=F32) for h in hs]
        sps = [jnp.maximum(z, 0.0) + jnp.log(1.0 + jnp.exp(-jnp.abs(z))) for z in zs]
        lombs = [jnp.where(strict, -sp, 0.0) if masked else -sp for sp in sps]
        his = [lomb.astype(BF16) for lomb in lombs]
        los = [(lomb - hi.astype(F32)).astype(BF16) for lomb, hi in zip(lombs, his)]
        tails = [jnp.dot(hi, later, preferred_element_type=F32) + jnp.dot(lo, later, preferred_element_type=F32)
                 for hi, lo in zip(his, los)]
        probs = [jnp.exp(zs[h] - sps[h] + tails[h] + carries[h][0]) for h in hs]
        if masked:
            probs = [jnp.where(strict, a, 0.0) for a in probs]
        accs = [carries[h][1] + jnp.dot(probs[h].astype(BF16), v_ref[pl.ds(k0, tq), _head_cols(h)],
                                        preferred_element_type=F32) for h in hs]
        runs = [carries[h][0] + jnp.sum(lombs[h], axis=-1, keepdims=True) for h in hs]
        return tuple(zip(runs, accs))

    def qbody(qi, _):
        q0 = pl.multiple_of(qi * tq, tq)
        qs = [q_ref[pl.ds(q0, tq), _head_cols(h)] for h in range(heads)]
        carries = tuple((jnp.zeros((tq, 1), F32), jnp.zeros((tq, HEAD_DIM), F32)) for _ in range(heads))
        carries = blocks(qs, q0, carries, True)

        def kbody(i, carries):
            return blocks(qs, pl.multiple_of((qi - 1 - i) * tq, tq), carries, False)

        carries = lax.fori_loop(0, qi, kbody, carries)
        for h in range(heads):
            o_ref[pl.ds(q0, tq), _head_cols(h)] = carries[h][1].astype(o_ref.dtype)
        return 0

    lax.fori_loop(0, nq, qbody, 0)


def sb_attention(qkv, nb, tq=256, heads=4):
    t, n3 = qkv.shape
    s = t // nb
    nh = n3 // (3 * HEAD_DIM)
    tq = _tile(s, tq)
    heads = min(heads, nh)
    ng = nh // heads
    cols = heads * HEAD_DIM
    return pl.pallas_call(
        functools.partial(_sb_attn_kernel, tq=tq, nq=s // tq, heads=heads),
        grid=(nb, ng),
        in_specs=[
            pl.BlockSpec((s, cols), lambda b, g: (b, g)),
            pl.BlockSpec((s, cols), lambda b, g: (b, ng + g)),
            pl.BlockSpec((s, cols), lambda b, g: (b, 2 * ng + g)),
        ],
        out_specs=pl.BlockSpec((s, cols), lambda b, g: (b, g)),
        out_shape=jax.ShapeDtypeStruct((t, nh * HEAD_DIM), BF16),
        compiler_params=_params("parallel", "parallel"),
        name="sb_attention",
    )(qkv, qkv, qkv)


def _mla_in_kernel(x_ref, sh_ref, sc_ref, w_ref, qn_ref, kvn_ref, cos_ref, sin_ref,
                   cq_ref, ckv_ref, kr_ref, *, q_rank, kv_rank):
    h = (x_ref[...] * (1.0 + sc_ref[0]) + sh_ref[0]).astype(BF16)
    lat = jnp.dot(h, w_ref[...], preferred_element_type=F32)

    def rms(v, g):
        return v * lax.rsqrt(jnp.mean(v * v, axis=-1, keepdims=True) + RMS_EPS) * g

    cq_ref[...] = rms(lat[:, :q_rank], qn_ref[...]).astype(cq_ref.dtype)
    ckv_ref[...] = rms(lat[:, q_rank:q_rank + kv_rank], kvn_ref[...]).astype(ckv_ref.dtype)
    kr_ref[...] = _rope(lat[:, q_rank + kv_rank:], cos_ref[...], sin_ref[...]).astype(kr_ref.dtype)


def mla_in(x, modr, layer, nb, w_in_pad, q_norm, kv_norm, cos, sin, tm=512):
    t, d = x.shape
    q_rank, kv_rank = q_norm.shape[0], kv_norm.shape[0]
    n = w_in_pad.shape[1]
    assert n == q_rank + kv_rank + LANES
    tm = _tile(t // nb, tm)
    tpb = (t // nb) // tm
    return pl.pallas_call(
        functools.partial(_mla_in_kernel, q_rank=q_rank, kv_rank=kv_rank),
        grid=(t // tm,),
        in_specs=[
            pl.BlockSpec((tm, d), lambda m: (m, 0)),
            _mod_spec(d, layer, nb, tpb, 0),
            _mod_spec(d, layer, nb, tpb, 1),
            pl.BlockSpec((d, n), lambda m: (0, 0)),
            pl.BlockSpec((1, q_rank), lambda m: (0, 0)),
            pl.BlockSpec((1, kv_rank), lambda m: (0, 0)),
            pl.BlockSpec((tm, LANES), lambda m: (m, 0)),
            pl.BlockSpec((tm, LANES), lambda m: (m, 0)),
        ],
        out_specs=[
            pl.BlockSpec((tm, q_rank), lambda m: (m, 0)),
            pl.BlockSpec((tm, kv_rank), lambda m: (m, 0)),
            pl.BlockSpec((tm, LANES), lambda m: (m, 0)),
        ],
        out_shape=[
            jax.ShapeDtypeStruct((t, q_rank), BF16),
            jax.ShapeDtypeStruct((t, kv_rank), BF16),
            jax.ShapeDtypeStruct((t, LANES), BF16),
        ],
        compiler_params=_params("parallel"),
        name="mla_in",
    )(x, modr, modr, w_in_pad, q_norm.reshape(1, q_rank), kv_norm.reshape(1, kv_rank), cos, sin)


def _mm_kernel(a_ref, w_ref, o_ref):
    o_ref[...] = jnp.dot(a_ref[...], w_ref[...], preferred_element_type=F32).astype(o_ref.dtype)


def mm(a, w, tm=1024, tn=1024):
    t, k = a.shape
    n = w.shape[1]
    tm, tn = _tile(t, tm), _tile(n, tn)
    return pl.pallas_call(
        _mm_kernel,
        grid=(t // tm, n // tn),
        in_specs=[pl.BlockSpec((tm, k), lambda m, j: (m, 0)), pl.BlockSpec((k, tn), lambda m, j: (0, j))],
        out_specs=pl.BlockSpec((tm, tn), lambda m, j: (m, j)),
        out_shape=jax.ShapeDtypeStruct((t, n), BF16),
        compiler_params=_params("parallel", "parallel"),
        name="mm",
    )(a, w)


def _mm_q_rope_kernel(a_ref, w_ref, cos_ref, sin_ref, o_ref, *, scale, heads):
    acc = jnp.dot(a_ref[...], w_ref[...], preferred_element_type=F32)
    cos, sin = cos_ref[...], sin_ref[...]
    for h in range(heads):
        c0 = 2 * LANES * h
        o_ref[:, c0:c0 + LANES] = (acc[:, c0:c0 + LANES] * scale).astype(o_ref.dtype)
        roped = _rope(acc[:, c0 + LANES:c0 + 2 * LANES], cos, sin)
        o_ref[:, c0 + LANES:c0 + 2 * LANES] = (roped * scale).astype(o_ref.dtype)


def mm_q_rope(a, w, cos, sin, scale, tm=1024, heads_per_tile=4):
    t, k = a.shape
    n = w.shape[1]
    tm = _tile(t, tm)
    tn = _tile(n, 2 * LANES * heads_per_tile)
    return pl.pallas_call(
        functools.partial(_mm_q_rope_kernel, scale=scale, heads=tn // (2 * LANES)),
        grid=(t // tm, n // tn),
        in_specs=[
            pl.BlockSpec((tm, k), lambda m, j: (m, 0)),
            pl.BlockSpec((k, tn), lambda m, j: (0, j)),
            pl.BlockSpec((tm, LANES), lambda m, j: (m, 0)),
            pl.BlockSpec((tm, LANES), lambda m, j: (m, 0)),
        ],
        out_specs=pl.BlockSpec((tm, tn), lambda m, j: (m, j)),
        out_shape=jax.ShapeDtypeStruct((t, n), BF16),
        compiler_params=_params("parallel", "parallel"),
        name="mm_q_rope",
    )(a, w, cos, sin)


def _mla_attn_kernel(q_ref, kv_ref, kr_ref, o_ref, kcat_ref, *, tq, nq, heads):
    for h in range(heads):
        c0 = 2 * HEAD_DIM * h
        kcat_ref[:, c0:c0 + HEAD_DIM] = kv_ref[:, c0:c0 + HEAD_DIM]
        kcat_ref[:, c0 + HEAD_DIM:c0 + 2 * HEAD_DIM] = kr_ref[...]
    row = lax.broadcasted_iota(jnp.int32, (tq, tq), 0)
    col = lax.broadcasted_iota(jnp.int32, (tq, tq), 1)
    causal = col <= row

    def blocks(qs, k0, carries, masked):
        hs = range(heads)
        ss = [lax.dot_general(qs[h], kcat_ref[pl.ds(k0, tq), 2 * HEAD_DIM * h:2 * HEAD_DIM * (h + 1)],
                              (((1,), (1,)), ((), ())), preferred_element_type=F32) for h in hs]
        if masked:
            ss = [jnp.where(causal, s, -jnp.inf) for s in ss]
        m_new = [jnp.maximum(carries[h][0], jnp.max(ss[h], axis=-1, keepdims=True)) for h in hs]
        ps = [jnp.exp(ss[h] - m_new[h]) for h in hs]
        corrs = [jnp.exp(carries[h][0] - m_new[h]) for h in hs]
        ls = [carries[h][1] * corrs[h] + jnp.sum(ps[h], axis=-1, keepdims=True) for h in hs]
        accs = [carries[h][2] * corrs[h]
                + jnp.dot(ps[h].astype(BF16), kv_ref[pl.ds(k0, tq), 2 * HEAD_DIM * h + HEAD_DIM:2 * HEAD_DIM * (h + 1)],
                          preferred_element_type=F32) for h in hs]
        return tuple(zip(m_new, ls, accs))

    def qbody(qi, _):
        q0 = pl.multiple_of(qi * tq, tq)
        qs = [q_ref[pl.ds(q0, tq), 2 * HEAD_DIM * h:2 * HEAD_DIM * (h + 1)] for h in range(heads)]
        carries = tuple((jnp.full((tq, 1), -jnp.inf, F32), jnp.zeros((tq, 1), F32), jnp.zeros((tq, HEAD_DIM), F32))
                        for _ in range(heads))
        carries = blocks(qs, q0, carries, True)

        def kbody(i, carries):
            return blocks(qs, pl.multiple_of(i * tq, tq), carries, False)

        carries = lax.fori_loop(0, qi, kbody, carries)
        for h in range(heads):
            _, l, acc = carries[h]
            o_ref[pl.ds(q0, tq), _head_cols(h)] = (acc / l).astype(o_ref.dtype)
        return 0

    lax.fori_loop(0, nq, qbody, 0)


def mla_attention(q, kv, kr, nb, tq=256, heads=4):
    t = q.shape[0]
    s = t // nb
    nh = q.shape[1] // (2 * HEAD_DIM)
    tq = _tile(s, tq)
    heads = min(heads, nh)
    return pl.pallas_call(
        functools.partial(_mla_attn_kernel, tq=tq, nq=s // tq, heads=heads),
        grid=(nb, nh // heads),
        in_specs=[
            pl.BlockSpec((s, 2 * HEAD_DIM * heads), lambda b, g: (b, g)),
            pl.BlockSpec((s, 2 * HEAD_DIM * heads), lambda b, g: (b, g)),
            pl.BlockSpec((s, LANES), lambda b, g: (b, 0)),
        ],
        out_specs=pl.BlockSpec((s, HEAD_DIM * heads), lambda b, g: (b, g)),
        out_shape=jax.ShapeDtypeStruct((t, nh * HEAD_DIM), BF16),
        scratch_shapes=[pltpu.VMEM((s, 2 * HEAD_DIM * heads), BF16)],
        compiler_params=_params("parallel", "parallel"),
        name="mla_attention",
    )(q, kv, kr)


def _router_kernel(x_ref, sh_ref, sc_ref, wr_ref, h_ref, idx_ref, gate_ref, cnt_ref, tri_ref, base_ref,
                   *, tm, n_exp):
    @pl.when(pl.program_id(0) == 0)
    def _():
        r = lax.broadcasted_iota(jnp.int32, (tm, tm), 0)
        c = lax.broadcasted_iota(jnp.int32, (tm, tm), 1)
        tri_ref[...] = (c <= r).astype(BF16)
        base_ref[...] = jnp.zeros_like(base_ref)

    h = x_ref[...] * (1.0 + sc_ref[0]) + sh_ref[0]
    for c in range(h_ref.shape[1]):
        h_ref[:, c, :] = h[:, c * LANES:(c + 1) * LANES]
    h_hi = h.astype(BF16)
    h_lo = (h - h_hi.astype(F32)).astype(BF16)
    w = wr_ref[...]
    w_hi = w.astype(BF16)
    w_lo = (w - w_hi.astype(F32)).astype(BF16)
    logits = (jnp.dot(h_hi, w_hi, preferred_element_type=F32)
              + jnp.dot(h_lo, w_hi, preferred_element_type=F32)
              + jnp.dot(h_hi, w_lo, preferred_element_type=F32))
    lane = lax.broadcasted_iota(jnp.int32, logits.shape, 1).astype(F32)
    lg = jnp.where(lane < n_exp, logits, -jnp.inf)
    m1 = jnp.max(lg, axis=-1, keepdims=True)
    i1 = jnp.min(jnp.where(lg == m1, lane, float(LANES)), axis=-1, keepdims=True)
    lg2 = jnp.where(lane == i1, -jnp.inf, lg)
    m2 = jnp.max(lg2, axis=-1, keepdims=True)
    i2 = jnp.min(jnp.where(lg2 == m2, lane, float(LANES)), axis=-1, keepdims=True)
    e = jnp.exp(m2 - m1)
    g1 = 1.0 / (1.0 + e)
    g2 = e * g1
    oh1 = lane == i1
    oh2 = lane == i2
    onehot = jnp.where(oh1 | oh2, 1.0, 0.0).astype(BF16)
    tot = base_ref[...] + jnp.dot(tri_ref[...], onehot, preferred_element_type=F32)
    r1 = jnp.sum(jnp.where(oh1, tot, 0.0), axis=-1, keepdims=True) - 1.0
    r2 = jnp.sum(jnp.where(oh2, tot, 0.0), axis=-1, keepdims=True) - 1.0
    last = tot[tm - 1:tm, :]
    base_ref[...] = last
    cnt_ref[...] = jnp.broadcast_to(last, cnt_ref.shape)
    info = jnp.where(lane == 0, i1, jnp.where(lane == 1, i2, jnp.where(lane == 2, r1, jnp.where(lane == 3, r2, 0.0))))
    idx_ref[...] = info.astype(jnp.int32)
    gate_ref[...] = jnp.where(lane == 0, g1, jnp.where(lane == 1, g2, 0.0))


def router(x, modr, layer, nb, w_router, tm=512):
    t, d = x.shape
    n_exp = w_router.shape[1]
    tm = _tile(t // nb, tm)
    tpb = (t // nb) // tm
    ns = d // LANES
    wr = jnp.pad(w_router, ((0, 0), (0, LANES - n_exp)))
    return pl.pallas_call(
        functools.partial(_router_kernel, tm=tm, n_exp=n_exp),
        grid=(t // tm,),
        in_specs=[
            pl.BlockSpec((tm, d), lambda m: (m, 0)),
            _mod_spec(d, layer, nb, tpb, 3),
            _mod_spec(d, layer, nb, tpb, 4),
            pl.BlockSpec((d, LANES), lambda m: (0, 0)),
        ],
        out_specs=[
            pl.BlockSpec((tm, ns, LANES), lambda m: (m, 0, 0)),
            pl.BlockSpec((tm, LANES), lambda m: (m, 0)),
            pl.BlockSpec((tm, LANES), lambda m: (m, 0)),
            pl.BlockSpec((8, LANES), lambda m: (0, 0)),
        ],
        out_shape=[
            jax.ShapeDtypeStruct((t, ns, LANES), F32),
            jax.ShapeDtypeStruct((t, LANES), jnp.int32),
            jax.ShapeDtypeStruct((t, LANES), F32),
            jax.ShapeDtypeStruct((8, LANES), F32),
        ],
        scratch_shapes=[pltpu.VMEM((tm, tm), BF16), pltpu.VMEM((1, LANES), F32)],
        compiler_params=_params("arbitrary"),
        name="router",
    )(x, modr, modr, wr)


def _gather_rows_kernel(idx_ref, src_ref, o_ref, buf_ref, sem, *, tg):
    def row_copy(r):
        return pltpu.make_async_copy(src_ref.at[idx_ref[0, 0, r]], buf_ref.at[r], sem)

    def start(r, _):
        row_copy(r).start()
        return 0

    def wait(r, _):
        row_copy(r).wait()
        return 0

    lax.fori_loop(0, tg, start, 0)
    lax.fori_loop(0, tg, wait, 0)
    for c in range(buf_ref.shape[1]):
        o_ref[:, c * LANES:(c + 1) * LANES] = buf_ref[:, c, :].astype(o_ref.dtype)


def gather_rows(src, idx, tg=512):
    n = idx.shape[0]
    ns = src.shape[1]
    tg = _tile(n, tg)
    return pl.pallas_call(
        functools.partial(_gather_rows_kernel, tg=tg),
        grid=(n // tg,),
        in_specs=[
            pl.BlockSpec((1, 1, tg), lambda m: (m, 0, 0), memory_space=pltpu.SMEM),
            pl.BlockSpec(memory_space=pl.ANY),
        ],
        out_specs=pl.BlockSpec((tg, ns * LANES), lambda m: (m, 0)),
        out_shape=jax.ShapeDtypeStruct((n, ns * LANES), BF16),
        scratch_shapes=[pltpu.VMEM((tg, ns, LANES), src.dtype), pltpu.SemaphoreType.DMA(())],
        compiler_params=_params("arbitrary"),
        name="gather_rows",
    )(idx.reshape(n // tg, 1, tg), src)


def _gmm_swiglu_kernel(te_ref, nu_ref, a_ref, wg_ref, wu_ref, o_ref, wgb_ref, wub_ref):
    m = pl.program_id(1)
    used = m < nu_ref[0]
    fresh = jnp.logical_or(m == 0, te_ref[m] != te_ref[jnp.maximum(m - 1, 0)])

    @pl.when(fresh)
    def _():
        wgb_ref[...] = wg_ref[...].astype(BF16)
        wub_ref[...] = wu_ref[...].astype(BF16)

    @pl.when(used)
    def _():
        a = a_ref[...]
        g = jnp.dot(a, wgb_ref[...], preferred_element_type=F32)
        u = jnp.dot(a, wub_ref[...], preferred_element_type=F32)
        o_ref[...] = (g * jax.nn.sigmoid(g) * u).astype(o_ref.dtype)

    @pl.when(jnp.logical_not(used))
    def _():
        o_ref[...] = jnp.zeros_like(o_ref)


def gmm_swiglu(a, w_gu, layer, tile_expert, n_used, tm, tn=512):
    ms, d = a.shape
    f = w_gu.shape[3] // 2
    tn = _tile(f, tn)
    nf = f // tn
    return pl.pallas_call(
        _gmm_swiglu_kernel,
        grid_spec=pltpu.PrefetchScalarGridSpec(
            num_scalar_prefetch=2,
            grid=(nf, ms // tm),
            in_specs=[
                pl.BlockSpec((tm, d), lambda j, m, te, nu: (m, 0)),
                pl.BlockSpec((None, None, d, tn), lambda j, m, te, nu: (layer, te[m], 0, j)),
                pl.BlockSpec((None, None, d, tn), lambda j, m, te, nu: (layer, te[m], 0, j + nf)),
            ],
            out_specs=pl.BlockSpec((tm, tn), lambda j, m, te, nu: (m, j)),
            scratch_shapes=[pltpu.VMEM((d, tn), BF16), pltpu.VMEM((d, tn), BF16)],
        ),
        out_shape=jax.ShapeDtypeStruct((ms, f), BF16),
        compiler_params=_params("parallel", "arbitrary"),
        name="gmm_swiglu",
    )(tile_expert, n_used, a, w_gu, w_gu)


def _gmm_kernel(te_ref, nu_ref, a_ref, w_ref, o_ref):
    used = pl.program_id(1) < nu_ref[0]

    @pl.when(used)
    def _():
        acc = jnp.dot(a_ref[...], w_ref[...], preferred_element_type=F32)
        for c in range(o_ref.shape[1]):
            o_ref[:, c, :] = acc[:, c * LANES:(c + 1) * LANES]

    @pl.when(jnp.logical_not(used))
    def _():
        o_ref[...] = jnp.zeros_like(o_ref)


def gmm(a, w, layer, tile_expert, n_used, tm, tn=1024):
    ms, k = a.shape
    n = w.shape[3]
    tn = _tile(n, tn, 8 * LANES)
    return pl.pallas_call(
        _gmm_kernel,
        grid_spec=pltpu.PrefetchScalarGridSpec(
            num_scalar_prefetch=2,
            grid=(n // tn, ms // tm),
            in_specs=[
                pl.BlockSpec((tm, k), lambda j, m, te, nu: (m, 0)),
                pl.BlockSpec((None, None, k, tn), lambda j, m, te, nu: (layer, te[m], 0, j)),
            ],
            out_specs=pl.BlockSpec((tm, tn // LANES, LANES), lambda j, m, te, nu: (m, j, 0)),
        ),
        out_shape=jax.ShapeDtypeStruct((ms, n // LANES, LANES), F32),
        compiler_params=_params("parallel", "arbitrary"),
        name="gmm",
    )(tile_expert, n_used, a, w)


def _combine_ln_kernel(p0_ref, p1_ref, y_ref, gate_ref, x_ref, gmod_ref, lng_ref, lnb_ref, o_ref,
                       y0_ref, y1_ref, ys_ref, sem, *, tm, alpha):
    def copies(r):
        return (pltpu.make_async_copy(y_ref.at[p0_ref[0, 0, r]], y0_ref.at[r], sem.at[0]),
                pltpu.make_async_copy(y_ref.at[p1_ref[0, 0, r]], y1_ref.at[r], sem.at[1]))

    def start(r, _):
        for cp in copies(r):
            cp.start()
        return 0

    def wait(r, _):
        for cp in copies(r):
            cp.wait()
        return 0

    lax.fori_loop(0, tm, start, 0)
    lax.fori_loop(0, tm, wait, 0)
    gates = gate_ref[...]
    g0, g1 = gates[:, 0:1], gates[:, 1:2]
    for c in range(y0_ref.shape[1]):
        ys_ref[:, c * LANES:(c + 1) * LANES] = g0 * y0_ref[:, c, :] + g1 * y1_ref[:, c, :]
    o_ref[...] = _res_ln(x_ref[...], ys_ref[...], gmod_ref[0], lng_ref[...], lnb_ref[...], alpha)


def combine_ln(y_slabs, pos0, pos1, gates, x, modr, layer, nb, ln_g, ln_b, alpha, tm=256):
    t, d = x.shape
    tm = _tile(t // nb, tm)
    tpb = (t // nb) // tm
    nt = t // tm
    ns = d // LANES
    return pl.pallas_call(
        functools.partial(_combine_ln_kernel, tm=tm, alpha=alpha),
        grid=(nt,),
        in_specs=[
            pl.BlockSpec((1, 1, tm), lambda m: (m, 0, 0), memory_space=pltpu.SMEM),
            pl.BlockSpec((1, 1, tm), lambda m: (m, 0, 0), memory_space=pltpu.SMEM),
            pl.BlockSpec(memory_space=pl.ANY),
            pl.BlockSpec((tm, LANES), lambda m: (m, 0)),
            pl.BlockSpec((tm, d), lambda m: (m, 0)),
            _mod_spec(d, layer, nb, tpb, 5),
            pl.BlockSpec((1, d), lambda m: (0, 0)),
            pl.BlockSpec((1, d), lambda m: (0, 0)),
        ],
        out_specs=pl.BlockSpec((tm, d), lambda m: (m, 0)),
        out_shape=jax.ShapeDtypeStruct((t, d), F32),
        scratch_shapes=[pltpu.VMEM((tm, ns, LANES), F32), pltpu.VMEM((tm, ns, LANES), F32),
                        pltpu.VMEM((tm, d), F32), pltpu.SemaphoreType.DMA((2,))],
        compiler_params=_params("arbitrary"),
        name="combine_ln",
    )(pos0.reshape(nt, 1, tm), pos1.reshape(nt, 1, tm), y_slabs, gates, x, modr, ln_g.reshape(1, d),
      ln_b.reshape(1, d))


def moe_layer(x, modr, layer, nb, w_router, w_gu, w_down, moe_index, ln_g, ln_b, alpha, tm_g=512):
    t, d = x.shape
    n_exp = w_router.shape[1]
    h, info, gates, cnt = router(x, modr, layer, nb, w_router)
    counts = cnt[0, :n_exp].astype(jnp.int32)
    sizes = (counts + tm_g - 1) // tm_g * tm_g
    ends = jnp.cumsum(sizes)
    starts = ends - sizes
    pos0 = starts[info[:, 0]] + info[:, 2]
    pos1 = starts[info[:, 1]] + info[:, 3]
    ms = t * TOP_K + n_exp * tm_g
    tok = jnp.arange(t, dtype=jnp.int32)
    slot_token = jnp.zeros((ms,), jnp.int32).at[pos0].set(tok).at[pos1].set(tok)
    tile_start = jnp.arange(ms // tm_g, dtype=jnp.int32) * tm_g
    tile_expert = jnp.minimum(jnp.sum(tile_start[:, None] >= ends[None, :], axis=1), n_exp - 1).astype(jnp.int32)
    n_used = (ends[-1:] // tm_g).astype(jnp.int32)

    h_sorted = gather_rows(h, slot_token)
    act = gmm_swiglu(h_sorted, w_gu, moe_index, tile_expert, n_used, tm_g)
    y_slabs = gmm(act, w_down, moe_index, tile_expert, n_used, tm_g)
    return combine_ln(y_slabs, pos0, pos1, gates, x, modr, layer, nb, ln_g, ln_b, alpha)


def kernel(x, c, positions, mod_w, mod_b, ln_g, ln_b, sb_w_qkv, sb_w_o, mla_w_in, mla_q_norm, mla_kv_norm,
           mla_w_uq, mla_w_ukv, mla_w_o, ffn_w_gu, ffn_w_down, moe_router, moe_w_gu, moe_w_down):
    nb, s, d = x.shape
    depth = mod_w.shape[0]
    t = nb * s
    alpha = float((2 * depth) ** 0.25)
    xf = x.reshape(t, d)

    modr = mod_table(c, mod_w, mod_b)
    cos, sin = rope_tables(positions)

    sb_heads = sb_w_o.shape[1] // HEAD_DIM
    sb_scale = HEAD_DIM ** -0.5
    sb_col_scale = jnp.concatenate([jnp.full((sb_heads * HEAD_DIM,), sb_scale, F32),
                                    jnp.ones((2 * sb_heads * HEAD_DIM,), F32)]).reshape(1, -1)

    mla_heads = mla_w_o.shape[1] // HEAD_DIM
    q_rank = mla_q_norm.shape[1]
    mla_scale = float((HEAD_DIM + ROPE_DIM) ** -0.5)
    moe_w_down_bf16 = moe_w_down.astype(BF16)

    for i in range(depth):
        j = i // 2
        if i % 2 == 0:
            qkv = modmm(xf, modr, i, nb, 0, sb_w_qkv[j].astype(BF16), sb_col_scale)
            o = sb_attention(qkv, nb)
            w_o = sb_w_o[j]
        else:
            w_in = jnp.pad(mla_w_in[j], ((0, 0), (0, LANES - ROPE_DIM))).astype(BF16)
            cq, ckv, kr = mla_in(xf, modr, i, nb, w_in, mla_q_norm[j], mla_kv_norm[j], cos, sin)
            w_uq = mla_w_uq[j].reshape(q_rank, mla_heads, HEAD_DIM + ROPE_DIM)
            w_uq = jnp.pad(w_uq, ((0, 0), (0, 0), (0, LANES - ROPE_DIM))).reshape(q_rank, mla_heads * 2 * LANES)
            q = mm_q_rope(cq, w_uq.astype(BF16), cos, sin, mla_scale)
            kv = mm(ckv, mla_w_ukv[j].astype(BF16))
            o = mla_attention(q, kv, kr, nb)
            w_o = mla_w_o[j]
        xf = mm_res_ln(o, w_o.astype(BF16), xf, modr, i, nb, 2, ln_g[i, 0], ln_b[i, 0], alpha)
        if i % 2 == 0:
            act = modmm_swiglu(xf, modr, i, nb, 3, ffn_w_gu[j].astype(BF16))
            xf = mm_res_ln(act, ffn_w_down[j].astype(BF16), xf, modr, i, nb, 5, ln_g[i, 1], ln_b[i, 1], alpha)
        else:
            xf = moe_layer(xf, modr, i, nb, moe_router[j], moe_w_gu, moe_w_down_bf16, j,
                           ln_g[i, 1], ln_b[i, 1], alpha)
    return xf.reshape(nb, s, d)
```

```python
import functools

import jax
import jax.numpy as jnp
from jax import lax
from jax.experimental import pallas as pl
from jax.experimental.pallas import tpu as pltpu

F32 = jnp.float32
BF16 = jnp.bfloat16

HEAD_DIM = 128
ROPE_DIM = 64
ROPE_HALF = ROPE_DIM // 2
ROPE_THETA = 10000.0
N_MOD = 6
TOP_K = 2
LN_EPS = 1e-5
RMS_EPS = 1e-6
LOG2_E = 1.4426950408889634
LANES = 128
VMEM_LIMIT = 56 * 1024 * 1024


def _tile(n, want, align=LANES):
    if n <= want:
        return n
    t = want - want % align
    while n % t:
        t -= align
    return t


def _params(*sem):
    return pltpu.CompilerParams(dimension_semantics=sem, vmem_limit_bytes=VMEM_LIMIT)


def _mod_kernel(c_ref, w_ref, b_ref, o_ref):
    c = c_ref[...]
    cond = (c * jax.nn.sigmoid(c)).astype(BF16)
    o_ref[...] = jnp.dot(cond, w_ref[...].astype(BF16), preferred_element_type=F32) + b_ref[...]


def mod_table(c, mod_w, mod_b):
    depth, d, n = mod_w.shape
    b = c.shape[0]
    tn = _tile(n, 1024)
    out = pl.pallas_call(
        _mod_kernel,
        grid=(depth, n // tn),
        in_specs=[
            pl.BlockSpec((b, d), lambda i, j: (0, 0)),
            pl.BlockSpec((None, d, tn), lambda i, j: (i, 0, j)),
            pl.BlockSpec((None, 1, tn), lambda i, j: (i, 0, j)),
        ],
        out_specs=pl.BlockSpec((None, b, tn), lambda i, j: (i, 0, j)),
        out_shape=jax.ShapeDtypeStruct((depth, b, n), F32),
        compiler_params=_params("parallel", "parallel"),
        name="mod_table",
    )(c, mod_w, mod_b.reshape(depth, 1, n))
    return out.reshape(depth * b * N_MOD, 1, d)


def _mod_spec(d, layer, nb, tiles_per_batch, chunk):
    def index(m, *_):
        return ((layer * nb + m // tiles_per_batch) * N_MOD + chunk, 0, 0)
    return pl.BlockSpec((1, 1, d), index)


def _rope_table_kernel(pos_ref, invf_ref, cos_ref, sin_ref):
    ang = pos_ref[...].astype(F32) * invf_ref[...]
    lane = lax.broadcasted_iota(jnp.int32, ang.shape, 1)
    live = lane < ROPE_DIM
    cos_ref[...] = jnp.where(live, jnp.cos(ang), 0.0)
    sin_ref[...] = jnp.where(live, jnp.sin(ang), 0.0)


def rope_tables(positions):
    t = positions.size
    tm = _tile(t, 1024)
    inv_freq = ROPE_THETA ** (-jnp.arange(ROPE_HALF, dtype=F32) / ROPE_HALF)
    invf = jnp.concatenate([inv_freq, inv_freq, jnp.zeros((LANES - ROPE_DIM,), F32)]).reshape(1, LANES)
    return pl.pallas_call(
        _rope_table_kernel,
        grid=(t // tm,),
        in_specs=[pl.BlockSpec((tm, 1), lambda m: (m, 0)), pl.BlockSpec((1, LANES), lambda m: (0, 0))],
        out_specs=[pl.BlockSpec((tm, LANES), lambda m: (m, 0))] * 2,
        out_shape=[jax.ShapeDtypeStruct((t, LANES), F32)] * 2,
        compiler_params=_params("parallel"),
        name="rope_tables",
    )(positions.reshape(t, 1), invf)


def _rope(x, cos, sin):
    lane = lax.broadcasted_iota(jnp.int32, x.shape, 1)
    rot = jnp.where(lane < ROPE_HALF, -pltpu.roll(x, LANES - ROPE_HALF, 1), pltpu.roll(x, ROPE_HALF, 1))
    return x * cos + rot * sin


def _modulate_into(h_ref, x_ref, sh_ref, sc_ref):
    @pl.when(pl.program_id(1) == 0)
    def _():
        h_ref[...] = (x_ref[...] * (1.0 + sc_ref[0]) + sh_ref[0]).astype(h_ref.dtype)


def _modmm_kernel(x_ref, sh_ref, sc_ref, w_ref, cs_ref, o_ref, h_ref):
    _modulate_into(h_ref, x_ref, sh_ref, sc_ref)
    acc = jnp.dot(h_ref[...], w_ref[...], preferred_element_type=F32)
    o_ref[...] = (acc * cs_ref[...]).astype(o_ref.dtype)


def modmm(x, modr, layer, nb, shift_chunk, w, col_scale, tm=1024, tn=1024):
    t, d = x.shape
    n = w.shape[1]
    tm, tn = _tile(t // nb, tm), _tile(n, tn)
    tpb = (t // nb) // tm
    return pl.pallas_call(
        _modmm_kernel,
        grid=(t // tm, n // tn),
        in_specs=[
            pl.BlockSpec((tm, d), lambda m, j: (m, 0)),
            _mod_spec(d, layer, nb, tpb, shift_chunk),
            _mod_spec(d, layer, nb, tpb, shift_chunk + 1),
            pl.BlockSpec((d, tn), lambda m, j: (0, j)),
            pl.BlockSpec((1, tn), lambda m, j: (0, j)),
        ],
        out_specs=pl.BlockSpec((tm, tn), lambda m, j: (m, j)),
        out_shape=jax.ShapeDtypeStruct((t, n), BF16),
        scratch_shapes=[pltpu.VMEM((tm, d), BF16)],
        compiler_params=_params("parallel", "arbitrary"),
        name="modmm",
    )(x, modr, modr, w, col_scale)


def _modmm_swiglu_kernel(x_ref, sh_ref, sc_ref, wg_ref, wu_ref, o_ref, h_ref):
    _modulate_into(h_ref, x_ref, sh_ref, sc_ref)
    h = h_ref[...]
    g = jnp.dot(h, wg_ref[...], preferred_element_type=F32)
    u = jnp.dot(h, wu_ref[...], preferred_element_type=F32)
    o_ref[...] = (g * jax.nn.sigmoid(g) * u).astype(o_ref.dtype)


def modmm_swiglu(x, modr, layer, nb, shift_chunk, w_gu, tm=1024, tn=512):
    t, d = x.shape
    f = w_gu.shape[1] // 2
    tm, tn = _tile(t // nb, tm), _tile(f, tn)
    tpb = (t // nb) // tm
    nf = f // tn
    return pl.pallas_call(
        _modmm_swiglu_kernel,
        grid=(t // tm, nf),
        in_specs=[
            pl.BlockSpec((tm, d), lambda m, j: (m, 0)),
            _mod_spec(d, layer, nb, tpb, shift_chunk),
            _mod_spec(d, layer, nb, tpb, shift_chunk + 1),
            pl.BlockSpec((d, tn), lambda m, j: (0, j)),
            pl.BlockSpec((d, tn), lambda m, j: (0, j + nf)),
        ],
        out_specs=pl.BlockSpec((tm, tn), lambda m, j: (m, j)),
        out_shape=jax.ShapeDtypeStruct((t, f), BF16),
        scratch_shapes=[pltpu.VMEM((tm, d), BF16)],
        compiler_params=_params("parallel", "arbitrary"),
        name="modmm_swiglu",
    )(x, modr, modr, w_gu, w_gu)


def _res_ln(x, y, gate, ln_g, ln_b, alpha):
    v = alpha * x + (1.0 + gate) * y
    mu = jnp.mean(v, axis=-1, keepdims=True)
    dv = v - mu
    var = jnp.mean(dv * dv, axis=-1, keepdims=True)
    return dv * lax.rsqrt(var + LN_EPS) * ln_g + ln_b


def _mm_res_ln_kernel(a_ref, w_ref, x_ref, gate_ref, lng_ref, lnb_ref, o_ref, acc_ref, *, alpha, nk):
    k = pl.program_id(1)
    part = jnp.dot(a_ref[...], w_ref[...], preferred_element_type=F32)

    @pl.when(k == 0)
    def _():
        acc_ref[...] = part

    @pl.when(k > 0)
    def _():
        acc_ref[...] += part

    @pl.when(k == nk - 1)
    def _():
        o_ref[...] = _res_ln(x_ref[...], acc_ref[...], gate_ref[0], lng_ref[...], lnb_ref[...], alpha)


def mm_res_ln(a, w, x, modr, layer, nb, gate_chunk, ln_g, ln_b, alpha, tm=512, tk=2048):
    t, kdim = a.shape
    d = w.shape[1]
    tm, tk = _tile(t // nb, tm), _tile(kdim, tk)
    tpb = (t // nb) // tm
    nk = kdim // tk
    return pl.pallas_call(
        functools.partial(_mm_res_ln_kernel, alpha=alpha, nk=nk),
        grid=(t // tm, nk),
        in_specs=[
            pl.BlockSpec((tm, tk), lambda m, k: (m, k)),
            pl.BlockSpec((tk, d), lambda m, k: (k, 0)),
            pl.BlockSpec((tm, d), lambda m, k: (m, 0)),
            _mod_spec(d, layer, nb, tpb, gate_chunk),
            pl.BlockSpec((1, d), lambda m, k: (0, 0)),
            pl.BlockSpec((1, d), lambda m, k: (0, 0)),
        ],
        out_specs=pl.BlockSpec((tm, d), lambda m, k: (m, 0)),
        out_shape=jax.ShapeDtypeStruct((t, d), F32),
        scratch_shapes=[pltpu.VMEM((tm, d), F32)],
        compiler_params=_params("parallel", "arbitrary"),
        name="mm_res_ln",
    )(a, w, x, modr, ln_g.reshape(1, d), ln_b.reshape(1, d))


def _head_cols(h):
    return slice(h * HEAD_DIM, (h + 1) * HEAD_DIM)


def _sb_attn_kernel(q_ref, k_ref, v_ref, o_ref, *, tq, nq, heads):
    row = lax.broadcasted_iota(jnp.int32, (tq, tq), 0)
    col = lax.broadcasted_iota(jnp.int32, (tq, tq), 1)
    strict = col < row
    later = strict.astype(BF16)
    later2 = jnp.concatenate([later, later], axis=0)

    def blocks(qs, k0, carries, masked):
        hs = range(heads)
        zs = [lax.dot_general(qs[h], k_ref[pl.ds(k0, tq), _head_cols(h)], (((1,), (1,)), ((), ())),
                              preferred_element_type=F32) for h in hs]
        bases, splits, sums = [], [], []
        for z in zs:
            sp = jnp.maximum(z, 0.0) + jnp.log(1.0 + jnp.exp2(-jnp.abs(z))) * LOG2_E
            bases.append(z - sp)
            if masked:
                sp = jnp.where(strict, sp, 0.0)
            hi = sp.astype(BF16)
            lo = (sp - hi.astype(F32)).astype(BF16)
            splits.append(jnp.concatenate([hi, lo], axis=1))
            sums.append(jnp.sum(sp, axis=-1, keepdims=True))
        tails = [jnp.dot(split, later2, preferred_element_type=F32) for split in splits]
        probs = [jnp.exp2(bases[h] - tails[h] - carries[h][0]) for h in hs]
        if masked:
            probs = [jnp.where(strict, a, 0.0) for a in probs]
        accs = [carries[h][1] + jnp.dot(probs[h].astype(BF16), v_ref[pl.ds(k0, tq), _head_cols(h)],
                                        preferred_element_type=F32) for h in hs]
        runs = [carries[h][0] + sums[h] for h in hs]
        return tuple(zip(runs, accs))

    def qbody(qi, _):
        q0 = pl.multiple_of(qi * tq, tq)
        qs = [q_ref[pl.ds(q0, tq), _head_cols(h)] for h in range(heads)]
        carries = tuple((jnp.zeros((tq, 1), F32), jnp.zeros((tq, HEAD_DIM), F32)) for _ in range(heads))
        carries = blocks(qs, q0, carries, True)

        def kbody(i, carries):
            return blocks(qs, pl.multiple_of((qi - 1 - i) * tq, tq), carries, False)

        carries = lax.fori_loop(0, qi, kbody, carries)
        for h in range(heads):
            o_ref[pl.ds(q0, tq), _head_cols(h)] = carries[h][1].astype(o_ref.dtype)
        return 0

    lax.fori_loop(0, nq, qbody, 0)


def sb_attention(qkv, nb, tq=256, heads=4):
    t, n3 = qkv.shape
    s = t // nb
    nh = n3 // (3 * HEAD_DIM)
    tq = _tile(s, tq)
    heads = min(heads, nh)
    ng = nh // heads
    cols = heads * HEAD_DIM
    return pl.pallas_call(
        functools.partial(_sb_attn_kernel, tq=tq, nq=s // tq, heads=heads),
        grid=(nb, ng),
        in_specs=[
            pl.BlockSpec((s, cols), lambda b, g: (b, g)),
            pl.BlockSpec((s, cols), lambda b, g: (b, ng + g)),
            pl.BlockSpec((s, cols), lambda b, g: (b, 2 * ng + g)),
        ],
        out_specs=pl.BlockSpec((s, cols), lambda b, g: (b, g)),
        out_shape=jax.ShapeDtypeStruct((t, nh * HEAD_DIM), BF16),
        compiler_params=_params("parallel", "parallel"),
        name="sb_attention",
    )(qkv, qkv, qkv)


def _mla_in_kernel(x_ref, sh_ref, sc_ref, w_ref, qn_ref, kvn_ref, cos_ref, sin_ref,
                   cq_ref, ckv_ref, kr_ref, *, q_rank, kv_rank):
    h = (x_ref[...] * (1.0 + sc_ref[0]) + sh_ref[0]).astype(BF16)
    lat = jnp.dot(h, w_ref[...], preferred_element_type=F32)

    def rms(v, g):
        return v * lax.rsqrt(jnp.mean(v * v, axis=-1, keepdims=True) + RMS_EPS) * g

    cq_ref[...] = rms(lat[:, :q_rank], qn_ref[...]).astype(cq_ref.dtype)
    ckv_ref[...] = rms(lat[:, q_rank:q_rank + kv_rank], kvn_ref[...]).astype(ckv_ref.dtype)
    kr_ref[...] = _rope(lat[:, q_rank + kv_rank:], cos_ref[...], sin_ref[...]).astype(kr_ref.dtype)


def mla_in(x, modr, layer, nb, w_in_pad, q_norm, kv_norm, cos, sin, tm=512):
    t, d = x.shape
    q_rank, kv_rank = q_norm.shape[0], kv_norm.shape[0]
    n = w_in_pad.shape[1]
    assert n == q_rank + kv_rank + LANES
    tm = _tile(t // nb, tm)
    tpb = (t // nb) // tm
    return pl.pallas_call(
        functools.partial(_mla_in_kernel, q_rank=q_rank, kv_rank=kv_rank),
        grid=(t // tm,),
        in_specs=[
            pl.BlockSpec((tm, d), lambda m: (m, 0)),
            _mod_spec(d, layer, nb, tpb, 0),
            _mod_spec(d, layer, nb, tpb, 1),
            pl.BlockSpec((d, n), lambda m: (0, 0)),
            pl.BlockSpec((1, q_rank), lambda m: (0, 0)),
            pl.BlockSpec((1, kv_rank), lambda m: (0, 0)),
            pl.BlockSpec((tm, LANES), lambda m: (m, 0)),
            pl.BlockSpec((tm, LANES), lambda m: (m, 0)),
        ],
        out_specs=[
            pl.BlockSpec((tm, q_rank), lambda m: (m, 0)),
            pl.BlockSpec((tm, kv_rank), lambda m: (m, 0)),
            pl.BlockSpec((tm, LANES), lambda m: (m, 0)),
        ],
        out_shape=[
            jax.ShapeDtypeStruct((t, q_rank), BF16),
            jax.ShapeDtypeStruct((t, kv_rank), BF16),
            jax.ShapeDtypeStruct((t, LANES), BF16),
        ],
        compiler_params=_params("parallel"),
        name="mla_in",
    )(x, modr, modr, w_in_pad, q_norm.reshape(1, q_rank), kv_norm.reshape(1, kv_rank), cos, sin)


def _mm_kernel(a_ref, w_ref, o_ref):
    o_ref[...] = jnp.dot(a_ref[...], w_ref[...], preferred_element_type=F32).astype(o_ref.dtype)


def mm(a, w, tm=1024, tn=1024):
    t, k = a.shape
    n = w.shape[1]
    tm, tn = _tile(t, tm), _tile(n, tn)
    return pl.pallas_call(
        _mm_kernel,
        grid=(t // tm, n // tn),
        in_specs=[pl.BlockSpec((tm, k), lambda m, j: (m, 0)), pl.BlockSpec((k, tn), lambda m, j: (0, j))],
        out_specs=pl.BlockSpec((tm, tn), lambda m, j: (m, j)),
        out_shape=jax.ShapeDtypeStruct((t, n), BF16),
        compiler_params=_params("parallel", "parallel"),
        name="mm",
    )(a, w)


def _mm_q_rope_kernel(a_ref, w_ref, cos_ref, sin_ref, o_ref, *, scale, heads):
    acc = jnp.dot(a_ref[...], w_ref[...], preferred_element_type=F32)
    cos, sin = cos_ref[...], sin_ref[...]
    for h in range(heads):
        c0 = 2 * LANES * h
        o_ref[:, c0:c0 + LANES] = (acc[:, c0:c0 + LANES] * scale).astype(o_ref.dtype)
        roped = _rope(acc[:, c0 + LANES:c0 + 2 * LANES], cos, sin)
        o_ref[:, c0 + LANES:c0 + 2 * LANES] = (roped * scale).astype(o_ref.dtype)


def mm_q_rope(a, w, cos, sin, scale, tm=1024, heads_per_tile=4):
    t, k = a.shape
    n = w.shape[1]
    tm = _tile(t, tm)
    tn = _tile(n, 2 * LANES * heads_per_tile)
    return pl.pallas_call(
        functools.partial(_mm_q_rope_kernel, scale=scale, heads=tn // (2 * LANES)),
        grid=(t // tm, n // tn),
        in_specs=[
            pl.BlockSpec((tm, k), lambda m, j: (m, 0)),
            pl.BlockSpec((k, tn), lambda m, j: (0, j)),
            pl.BlockSpec((tm, LANES), lambda m, j: (m, 0)),
            pl.BlockSpec((tm, LANES), lambda m, j: (m, 0)),
        ],
        out_specs=pl.BlockSpec((tm, tn), lambda m, j: (m, j)),
        out_shape=jax.ShapeDtypeStruct((t, n), BF16),
        compiler_params=_params("parallel", "parallel"),
        name="mm_q_rope",
    )(a, w, cos, sin)


def _mla_attn_kernel(q_ref, kv_ref, kr_ref, o_ref, kcat_ref, *, tq, nq, heads):
    for h in range(heads):
        c0 = 2 * HEAD_DIM * h
        kcat_ref[:, c0:c0 + HEAD_DIM] = kv_ref[:, c0:c0 + HEAD_DIM]
        kcat_ref[:, c0 + HEAD_DIM:c0 + 2 * HEAD_DIM] = kr_ref[...]
    row = lax.broadcasted_iota(jnp.int32, (tq, tq), 0)
    col = lax.broadcasted_iota(jnp.int32, (tq, tq), 1)
    causal = col <= row

    def blocks(qs, k0, carries, masked):
        hs = range(heads)
        ss = [lax.dot_general(qs[h], kcat_ref[pl.ds(k0, tq), 2 * HEAD_DIM * h:2 * HEAD_DIM * (h + 1)],
                              (((1,), (1,)), ((), ())), preferred_element_type=F32) for h in hs]
        if masked:
            ss = [jnp.where(causal, s, -jnp.inf) for s in ss]
        m_new, ps, corrs, ls = [], [], [], []
        for h in hs:
            m_old, l_old, _ = carries[h]
            m = jnp.maximum(m_old, jnp.max(ss[h], axis=-1, keepdims=True))
            p = jnp.exp(ss[h] - m)
            corr = jnp.exp(m_old - m)
            m_new.append(m)
            corrs.append(corr)
            ls.append(l_old * corr + jnp.sum(p, axis=-1, keepdims=True))
            ps.append(p.astype(BF16))
        accs = [carries[h][2] * corrs[h]
                + jnp.dot(ps[h], kv_ref[pl.ds(k0, tq), 2 * HEAD_DIM * h + HEAD_DIM:2 * HEAD_DIM * (h + 1)],
                          preferred_element_type=F32) for h in hs]
        return tuple(zip(m_new, ls, accs))

    def qbody(qi, _):
        q0 = pl.multiple_of(qi * tq, tq)
        qs = [q_ref[pl.ds(q0, tq), 2 * HEAD_DIM * h:2 * HEAD_DIM * (h + 1)] for h in range(heads)]
        carries = tuple((jnp.full((tq, 1), -jnp.inf, F32), jnp.zeros((tq, 1), F32), jnp.zeros((tq, HEAD_DIM), F32))
                        for _ in range(heads))
        carries = blocks(qs, q0, carries, True)

        def kbody(i, carries):
            return blocks(qs, pl.multiple_of(i * tq, tq), carries, False)

        carries = lax.fori_loop(0, qi, kbody, carries)
        for h in range(heads):
            _, l, acc = carries[h]
            o_ref[pl.ds(q0, tq), _head_cols(h)] = (acc / l).astype(o_ref.dtype)
        return 0

    lax.fori_loop(0, nq, qbody, 0)


def mla_attention(q, kv, kr, nb, tq=256, heads=4):
    t = q.shape[0]
    s = t // nb
    nh = q.shape[1] // (2 * HEAD_DIM)
    tq = _tile(s, tq)
    heads = min(heads, nh)
    return pl.pallas_call(
        functools.partial(_mla_attn_kernel, tq=tq, nq=s // tq, heads=heads),
        grid=(nb, nh // heads),
        in_specs=[
            pl.BlockSpec((s, 2 * HEAD_DIM * heads), lambda b, g: (b, g)),
            pl.BlockSpec((s, 2 * HEAD_DIM * heads), lambda b, g: (b, g)),
            pl.BlockSpec((s, LANES), lambda b, g: (b, 0)),
        ],
        out_specs=pl.BlockSpec((s, HEAD_DIM * heads), lambda b, g: (b, g)),
        out_shape=jax.ShapeDtypeStruct((t, nh * HEAD_DIM), BF16),
        scratch_shapes=[pltpu.VMEM((s, 2 * HEAD_DIM * heads), BF16)],
        compiler_params=_params("parallel", "parallel"),
        name="mla_attention",
    )(q, kv, kr)


def _router_kernel(x_ref, sh_ref, sc_ref, wr_ref, h_ref, idx_ref, gate_ref, cnt_ref, tri_ref, base_ref,
                   *, tm, n_exp):
    @pl.when(pl.program_id(0) == 0)
    def _():
        r = lax.broadcasted_iota(jnp.int32, (tm, tm), 0)
        c = lax.broadcasted_iota(jnp.int32, (tm, tm), 1)
        tri_ref[...] = (c <= r).astype(BF16)
        base_ref[...] = jnp.zeros_like(base_ref)

    h = x_ref[...] * (1.0 + sc_ref[0]) + sh_ref[0]
    for c in range(h_ref.shape[1]):
        h_ref[:, c, :] = h[:, c * LANES:(c + 1) * LANES]
    h_hi = h.astype(BF16)
    h_lo = (h - h_hi.astype(F32)).astype(BF16)
    w = wr_ref[...]
    w_hi = w.astype(BF16)
    w_lo = (w - w_hi.astype(F32)).astype(BF16)
    logits = (jnp.dot(h_hi, w_hi, preferred_element_type=F32)
              + jnp.dot(h_lo, w_hi, preferred_element_type=F32)
              + jnp.dot(h_hi, w_lo, preferred_element_type=F32))
    lane = lax.broadcasted_iota(jnp.int32, logits.shape, 1).astype(F32)
    lg = jnp.where(lane < n_exp, logits, -jnp.inf)
    m1 = jnp.max(lg, axis=-1, keepdims=True)
    i1 = jnp.min(jnp.where(lg == m1, lane, float(LANES)), axis=-1, keepdims=True)
    lg2 = jnp.where(lane == i1, -jnp.inf, lg)
    m2 = jnp.max(lg2, axis=-1, keepdims=True)
    i2 = jnp.min(jnp.where(lg2 == m2, lane, float(LANES)), axis=-1, keepdims=True)
    e = jnp.exp(m2 - m1)
    g1 = 1.0 / (1.0 + e)
    g2 = e * g1
    oh1 = lane == i1
    oh2 = lane == i2
    onehot = jnp.where(oh1 | oh2, 1.0, 0.0).astype(BF16)
    tot = base_ref[...] + jnp.dot(tri_ref[...], onehot, preferred_element_type=F32)
    r1 = jnp.sum(jnp.where(oh1, tot, 0.0), axis=-1, keepdims=True) - 1.0
    r2 = jnp.sum(jnp.where(oh2, tot, 0.0), axis=-1, keepdims=True) - 1.0
    last = tot[tm - 1:tm, :]
    base_ref[...] = last
    cnt_ref[...] = jnp.broadcast_to(last, cnt_ref.shape)
    info = jnp.where(lane == 0, i1, jnp.where(lane == 1, i2, jnp.where(lane == 2, r1, jnp.where(lane == 3, r2, 0.0))))
    idx_ref[...] = info.astype(jnp.int32)
    gate_ref[...] = jnp.where(lane == 0, g1, jnp.where(lane == 1, g2, 0.0))


def router(x, modr, layer, nb, w_router, tm=512):
    t, d = x.shape
    n_exp = w_router.shape[1]
    tm = _tile(t // nb, tm)
    tpb = (t // nb) // tm
    ns = d // LANES
    wr = jnp.pad(w_router, ((0, 0), (0, LANES - n_exp)))
    return pl.pallas_call(
        functools.partial(_router_kernel, tm=tm, n_exp=n_exp),
        grid=(t // tm,),
        in_specs=[
            pl.BlockSpec((tm, d), lambda m: (m, 0)),
            _mod_spec(d, layer, nb, tpb, 3),
            _mod_spec(d, layer, nb, tpb, 4),
            pl.BlockSpec((d, LANES), lambda m: (0, 0)),
        ],
        out_specs=[
            pl.BlockSpec((tm, ns, LANES), lambda m: (m, 0, 0)),
            pl.BlockSpec((tm, LANES), lambda m: (m, 0)),
            pl.BlockSpec((tm, LANES), lambda m: (m, 0)),
            pl.BlockSpec((8, LANES), lambda m: (0, 0)),
        ],
        out_shape=[
            jax.ShapeDtypeStruct((t, ns, LANES), F32),
            jax.ShapeDtypeStruct((t, LANES), jnp.int32),
            jax.ShapeDtypeStruct((t, LANES), F32),
            jax.ShapeDtypeStruct((8, LANES), F32),
        ],
        scratch_shapes=[pltpu.VMEM((tm, tm), BF16), pltpu.VMEM((1, LANES), F32)],
        compiler_params=_params("arbitrary"),
        name="router",
    )(x, modr, modr, wr)


ROW_DMA_UNROLL = 8


def _gather_rows_kernel(idx_ref, idx_next_ref, src_ref, o_ref, buf_ref, sem, *, tg, steps):
    i = pl.program_id(0)
    slot = i % 2

    def issue(ids_ref, s):
        def start(r, _):
            pltpu.make_async_copy(src_ref.at[ids_ref[0, 0, r]], buf_ref.at[s, r], sem.at[s]).start()
            return 0
        lax.fori_loop(0, tg, start, 0, unroll=ROW_DMA_UNROLL)

    @pl.when(i == 0)
    def _():
        issue(idx_ref, 0)

    @pl.when(i + 1 < steps)
    def _():
        issue(idx_next_ref, 1 - slot)

    pltpu.make_async_copy(src_ref.at[pl.ds(0, tg)], buf_ref.at[slot], sem.at[slot]).wait()
    for c in range(buf_ref.shape[2]):
        o_ref[:, c * LANES:(c + 1) * LANES] = buf_ref[slot, :, c, :].astype(o_ref.dtype)


def gather_rows(src, idx, tg=512):
    n = idx.shape[0]
    ns = src.shape[1]
    tg = _tile(n, tg)
    steps = n // tg
    ids = idx.reshape(steps, 1, tg)
    return pl.pallas_call(
        functools.partial(_gather_rows_kernel, tg=tg, steps=steps),
        grid=(steps,),
        in_specs=[
            pl.BlockSpec((1, 1, tg), lambda m: (m, 0, 0), memory_space=pltpu.SMEM),
            pl.BlockSpec((1, 1, tg), lambda m: (jnp.minimum(m + 1, steps - 1), 0, 0), memory_space=pltpu.SMEM),
            pl.BlockSpec(memory_space=pl.ANY),
        ],
        out_specs=pl.BlockSpec((tg, ns * LANES), lambda m: (m, 0)),
        out_shape=jax.ShapeDtypeStruct((n, ns * LANES), BF16),
        scratch_shapes=[pltpu.VMEM((2, tg, ns, LANES), src.dtype), pltpu.SemaphoreType.DMA((2,))],
        compiler_params=_params("arbitrary"),
        name="gather_rows",
    )(ids, ids, src)


def _gmm_swiglu_kernel(te_ref, nu_ref, a_ref, wg_ref, wu_ref, o_ref, wgb_ref, wub_ref):
    m = pl.program_id(1)
    used = m < nu_ref[0]
    fresh = jnp.logical_or(m == 0, te_ref[m] != te_ref[jnp.maximum(m - 1, 0)])

    @pl.when(fresh)
    def _():
        wgb_ref[...] = wg_ref[...].astype(BF16)
        wub_ref[...] = wu_ref[...].astype(BF16)

    @pl.when(used)
    def _():
        a = a_ref[...]
        g = jnp.dot(a, wgb_ref[...], preferred_element_type=F32)
        u = jnp.dot(a, wub_ref[...], preferred_element_type=F32)
        o_ref[...] = (g * jax.nn.sigmoid(g) * u).astype(o_ref.dtype)

    @pl.when(jnp.logical_not(used))
    def _():
        o_ref[...] = jnp.zeros_like(o_ref)


def gmm_swiglu(a, w_gu, layer, tile_expert, n_used, tm, tn=1024):
    ms, d = a.shape
    f = w_gu.shape[3] // 2
    tn = _tile(f, tn)
    nf = f // tn
    return pl.pallas_call(
        _gmm_swiglu_kernel,
        grid_spec=pltpu.PrefetchScalarGridSpec(
            num_scalar_prefetch=2,
            grid=(nf, ms // tm),
            in_specs=[
                pl.BlockSpec((tm, d), lambda j, m, te, nu: (m, 0)),
                pl.BlockSpec((None, None, d, tn), lambda j, m, te, nu: (layer, te[m], 0, j)),
                pl.BlockSpec((None, None, d, tn), lambda j, m, te, nu: (layer, te[m], 0, j + nf)),
            ],
            out_specs=pl.BlockSpec((tm, tn), lambda j, m, te, nu: (m, j)),
            scratch_shapes=[pltpu.VMEM((d, tn), BF16), pltpu.VMEM((d, tn), BF16)],
        ),
        out_shape=jax.ShapeDtypeStruct((ms, f), BF16),
        compiler_params=_params("parallel", "arbitrary"),
        name="gmm_swiglu",
    )(tile_expert, n_used, a, w_gu, w_gu)


def _gmm_kernel(te_ref, nu_ref, a_ref, w_ref, o_ref):
    used = pl.program_id(1) < nu_ref[0]

    @pl.when(used)
    def _():
        acc = jnp.dot(a_ref[...], w_ref[...], preferred_element_type=F32)
        for c in range(o_ref.shape[1]):
            o_ref[:, c, :] = acc[:, c * LANES:(c + 1) * LANES]

    @pl.when(jnp.logical_not(used))
    def _():
        o_ref[...] = jnp.zeros_like(o_ref)


def gmm(a, w, layer, tile_expert, n_used, tm, tn=1024):
    ms, k = a.shape
    n = w.shape[3]
    tn = _tile(n, tn, 8 * LANES)
    return pl.pallas_call(
        _gmm_kernel,
        grid_spec=pltpu.PrefetchScalarGridSpec(
            num_scalar_prefetch=2,
            grid=(n // tn, ms // tm),
            in_specs=[
                pl.BlockSpec((tm, k), lambda j, m, te, nu: (m, 0)),
                pl.BlockSpec((None, None, k, tn), lambda j, m, te, nu: (layer, te[m], 0, j)),
            ],
            out_specs=pl.BlockSpec((tm, tn // LANES, LANES), lambda j, m, te, nu: (m, j, 0)),
        ),
        out_shape=jax.ShapeDtypeStruct((ms, n // LANES, LANES), F32),
        compiler_params=_params("parallel", "arbitrary"),
        name="gmm",
    )(tile_expert, n_used, a, w)


def _combine_ln_kernel(p0_ref, p1_ref, p0_next_ref, p1_next_ref, y_ref, gate_ref, x_ref, gmod_ref, lng_ref, lnb_ref,
                       o_ref, ybuf_ref, ys_ref, sem, *, tm, steps, alpha):
    i = pl.program_id(0)
    slot = i % 2

    def issue(pa_ref, pb_ref, s):
        def start(r, _):
            pltpu.make_async_copy(y_ref.at[pa_ref[0, 0, r]], ybuf_ref.at[s, 0, r], sem.at[s]).start()
            pltpu.make_async_copy(y_ref.at[pb_ref[0, 0, r]], ybuf_ref.at[s, 1, r], sem.at[s]).start()
            return 0
        lax.fori_loop(0, tm, start, 0, unroll=ROW_DMA_UNROLL)

    @pl.when(i == 0)
    def _():
        issue(p0_ref, p1_ref, 0)

    @pl.when(i + 1 < steps)
    def _():
        issue(p0_next_ref, p1_next_ref, 1 - slot)

    for k in range(TOP_K):
        pltpu.make_async_copy(y_ref.at[pl.ds(0, tm)], ybuf_ref.at[slot, k], sem.at[slot]).wait()
    gates = gate_ref[...]
    g0, g1 = gates[:, 0:1], gates[:, 1:2]
    for c in range(ybuf_ref.shape[3]):
        ys_ref[:, c * LANES:(c + 1) * LANES] = g0 * ybuf_ref[slot, 0, :, c, :] + g1 * ybuf_ref[slot, 1, :, c, :]
    o_ref[...] = _res_ln(x_ref[...], ys_ref[...], gmod_ref[0], lng_ref[...], lnb_ref[...], alpha)


def combine_ln(y_slabs, pos0, pos1, gates, x, modr, layer, nb, ln_g, ln_b, alpha, tm=256):
    t, d = x.shape
    tm = _tile(t // nb, tm)
    tpb = (t // nb) // tm
    nt = t // tm
    ns = d // LANES
    ids_spec = pl.BlockSpec((1, 1, tm), lambda m: (m, 0, 0), memory_space=pltpu.SMEM)
    ids_next_spec = pl.BlockSpec((1, 1, tm), lambda m: (jnp.minimum(m + 1, nt - 1), 0, 0), memory_space=pltpu.SMEM)
    p0, p1 = pos0.reshape(nt, 1, tm), pos1.reshape(nt, 1, tm)
    return pl.pallas_call(
        functools.partial(_combine_ln_kernel, tm=tm, steps=nt, alpha=alpha),
        grid=(nt,),
        in_specs=[
            ids_spec,
            ids_spec,
            ids_next_spec,
            ids_next_spec,
            pl.BlockSpec(memory_space=pl.ANY),
            pl.BlockSpec((tm, LANES), lambda m: (m, 0)),
            pl.BlockSpec((tm, d), lambda m: (m, 0)),
            _mod_spec(d, layer, nb, tpb, 5),
            pl.BlockSpec((1, d), lambda m: (0, 0)),
            pl.BlockSpec((1, d), lambda m: (0, 0)),
        ],
        out_specs=pl.BlockSpec((tm, d), lambda m: (m, 0)),
        out_shape=jax.ShapeDtypeStruct((t, d), F32),
        scratch_shapes=[pltpu.VMEM((2, TOP_K, tm, ns, LANES), F32), pltpu.VMEM((tm, d), F32),
                        pltpu.SemaphoreType.DMA((2,))],
        compiler_params=_params("arbitrary"),
        name="combine_ln",
    )(p0, p1, p0, p1, y_slabs, gates, x, modr, ln_g.reshape(1, d), ln_b.reshape(1, d))


def moe_layer(x, modr, layer, nb, w_router, w_gu, w_down, moe_index, ln_g, ln_b, alpha, tm_g=512):
    t, d = x.shape
    n_exp = w_router.shape[1]
    h, info, gates, cnt = router(x, modr, layer, nb, w_router)
    counts = cnt[0, :n_exp].astype(jnp.int32)
    sizes = (counts + tm_g - 1) // tm_g * tm_g
    ends = jnp.cumsum(sizes)
    starts = ends - sizes
    pos0 = starts[info[:, 0]] + info[:, 2]
    pos1 = starts[info[:, 1]] + info[:, 3]
    ms = t * TOP_K + n_exp * tm_g
    tok = jnp.arange(t, dtype=jnp.int32)
    slot_token = jnp.zeros((ms,), jnp.int32).at[pos0].set(tok).at[pos1].set(tok)
    tile_start = jnp.arange(ms // tm_g, dtype=jnp.int32) * tm_g
    tile_expert = jnp.minimum(jnp.sum(tile_start[:, None] >= ends[None, :], axis=1), n_exp - 1).astype(jnp.int32)
    n_used = (ends[-1:] // tm_g).astype(jnp.int32)

    h_sorted = gather_rows(h, slot_token)
    act = gmm_swiglu(h_sorted, w_gu, moe_index, tile_expert, n_used, tm_g)
    y_slabs = gmm(act, w_down, moe_index, tile_expert, n_used, tm_g)
    return combine_ln(y_slabs, pos0, pos1, gates, x, modr, layer, nb, ln_g, ln_b, alpha)


def kernel(x, c, positions, mod_w, mod_b, ln_g, ln_b, sb_w_qkv, sb_w_o, mla_w_in, mla_q_norm, mla_kv_norm,
           mla_w_uq, mla_w_ukv, mla_w_o, ffn_w_gu, ffn_w_down, moe_router, moe_w_gu, moe_w_down):
    nb, s, d = x.shape
    depth = mod_w.shape[0]
    t = nb * s
    alpha = float((2 * depth) ** 0.25)
    xf = x.reshape(t, d)

    modr = mod_table(c, mod_w, mod_b)
    cos, sin = rope_tables(positions)

    sb_heads = sb_w_o.shape[1] // HEAD_DIM
    sb_scale = HEAD_DIM ** -0.5 * LOG2_E
    sb_col_scale = jnp.concatenate([jnp.full((sb_heads * HEAD_DIM,), sb_scale, F32),
                                    jnp.ones((2 * sb_heads * HEAD_DIM,), F32)]).reshape(1, -1)

    mla_heads = mla_w_o.shape[1] // HEAD_DIM
    q_rank = mla_q_norm.shape[1]
    mla_scale = float((HEAD_DIM + ROPE_DIM) ** -0.5)
    moe_w_down_bf16 = moe_w_down.astype(BF16)

    for i in range(depth):
        j = i // 2
        if i % 2 == 0:
            qkv = modmm(xf, modr, i, nb, 0, sb_w_qkv[j].astype(BF16), sb_col_scale)
            o = sb_attention(qkv, nb)
            w_o = sb_w_o[j]
        else:
            w_in = jnp.pad(mla_w_in[j], ((0, 0), (0, LANES - ROPE_DIM))).astype(BF16)
            cq, ckv, kr = mla_in(xf, modr, i, nb, w_in, mla_q_norm[j], mla_kv_norm[j], cos, sin)
            w_uq = mla_w_uq[j].reshape(q_rank, mla_heads, HEAD_DIM + ROPE_DIM)
            w_uq = jnp.pad(w_uq, ((0, 0), (0, 0), (0, LANES - ROPE_DIM))).reshape(q_rank, mla_heads * 2 * LANES)
            q = mm_q_rope(cq, w_uq.astype(BF16), cos, sin, mla_scale)
            kv = mm(ckv, mla_w_ukv[j].astype(BF16))
            o = mla_attention(q, kv, kr, nb)
            w_o = mla_w_o[j]
        xf = mm_res_ln(o, w_o.astype(BF16), xf, modr, i, nb, 2, ln_g[i, 0], ln_b[i, 0], alpha)
        if i % 2 == 0:
            act = modmm_swiglu(xf, modr, i, nb, 3, ffn_w_gu[j].astype(BF16))
            xf = mm_res_ln(act, ffn_w_down[j].astype(BF16), xf, modr, i, nb, 5, ln_g[i, 1], ln_b[i, 1], alpha)
        else:
            xf = moe_layer(xf, modr, i, nb, moe_router[j], moe_w_gu, moe_w_down_bf16, j,
                           ln_g[i, 1], ln_b[i, 1], alpha)
    return xf.reshape(nb, s, d)
```

```python
import functools

import jax
import jax.numpy as jnp
from jax import lax
from jax.experimental import pallas as pl
from jax.experimental.pallas import tpu as pltpu

F32 = jnp.float32
BF16 = jnp.bfloat16

HEAD_DIM = 128
ROPE_DIM = 64
ROPE_HALF = ROPE_DIM // 2
ROPE_THETA = 10000.0
N_MOD = 6
TOP_K = 2
LN_EPS = 1e-5
RMS_EPS = 1e-6
LOG2_E = 1.4426950408889634
ATTN_BLOCK = 512
LANES = 128
VMEM_LIMIT = 56 * 1024 * 1024


def _tile(n, want, align=LANES):
    if n <= want:
        return n
    t = want - want % align
    while n % t:
        t -= align
    return t


def _params(*sem):
    return pltpu.CompilerParams(dimension_semantics=sem, vmem_limit_bytes=VMEM_LIMIT)


def _mod_kernel(c_ref, w_ref, b_ref, o_ref):
    c = c_ref[...]
    cond = (c * jax.nn.sigmoid(c)).astype(BF16)
    o_ref[...] = jnp.dot(cond, w_ref[...].astype(BF16), preferred_element_type=F32) + b_ref[...]


def mod_table(c, mod_w, mod_b):
    depth, d, n = mod_w.shape
    b = c.shape[0]
    tn = _tile(n, 1024)
    out = pl.pallas_call(
        _mod_kernel,
        grid=(depth, n // tn),
        in_specs=[
            pl.BlockSpec((b, d), lambda i, j: (0, 0)),
            pl.BlockSpec((None, d, tn), lambda i, j: (i, 0, j)),
            pl.BlockSpec((None, 1, tn), lambda i, j: (i, 0, j)),
        ],
        out_specs=pl.BlockSpec((None, b, tn), lambda i, j: (i, 0, j)),
        out_shape=jax.ShapeDtypeStruct((depth, b, n), F32),
        compiler_params=_params("parallel", "parallel"),
        name="mod_table",
    )(c, mod_w, mod_b.reshape(depth, 1, n))
    return out.reshape(depth * b * N_MOD, 1, d)


def _mod_spec(d, layer, nb, tiles_per_batch, chunk):
    def index(m, *_):
        return ((layer * nb + m // tiles_per_batch) * N_MOD + chunk, 0, 0)
    return pl.BlockSpec((1, 1, d), index)


def _rope_table_kernel(pos_ref, invf_ref, cos_ref, sin_ref):
    ang = pos_ref[...].astype(F32) * invf_ref[...]
    lane = lax.broadcasted_iota(jnp.int32, ang.shape, 1)
    live = lane < ROPE_DIM
    cos_ref[...] = jnp.where(live, jnp.cos(ang), 0.0)
    sin_ref[...] = jnp.where(live, jnp.sin(ang), 0.0)


def rope_tables(positions):
    t = positions.size
    tm = _tile(t, 1024)
    inv_freq = ROPE_THETA ** (-jnp.arange(ROPE_HALF, dtype=F32) / ROPE_HALF)
    invf = jnp.concatenate([inv_freq, inv_freq, jnp.zeros((LANES - ROPE_DIM,), F32)]).reshape(1, LANES)
    return pl.pallas_call(
        _rope_table_kernel,
        grid=(t // tm,),
        in_specs=[pl.BlockSpec((tm, 1), lambda m: (m, 0)), pl.BlockSpec((1, LANES), lambda m: (0, 0))],
        out_specs=[pl.BlockSpec((tm, LANES), lambda m: (m, 0))] * 2,
        out_shape=[jax.ShapeDtypeStruct((t, LANES), F32)] * 2,
        compiler_params=_params("parallel"),
        name="rope_tables",
    )(positions.reshape(t, 1), invf)


def _rope(x, cos, sin):
    lane = lax.broadcasted_iota(jnp.int32, x.shape, 1)
    rot = jnp.where(lane < ROPE_HALF, -pltpu.roll(x, LANES - ROPE_HALF, 1), pltpu.roll(x, ROPE_HALF, 1))
    return x * cos + rot * sin


def _modulate_into(h_ref, x_ref, sh_ref, sc_ref):
    @pl.when(pl.program_id(1) == 0)
    def _():
        h_ref[...] = (x_ref[...] * (1.0 + sc_ref[0]) + sh_ref[0]).astype(h_ref.dtype)


def _modmm_kernel(x_ref, sh_ref, sc_ref, w_ref, cs_ref, o_ref, h_ref):
    _modulate_into(h_ref, x_ref, sh_ref, sc_ref)
    acc = jnp.dot(h_ref[...], w_ref[...], preferred_element_type=F32)
    o_ref[...] = (acc * cs_ref[...]).astype(o_ref.dtype)


def modmm(x, modr, layer, nb, shift_chunk, w, col_scale, tm=1024, tn=1024):
    t, d = x.shape
    n = w.shape[1]
    tm, tn = _tile(t // nb, tm), _tile(n, tn)
    tpb = (t // nb) // tm
    return pl.pallas_call(
        _modmm_kernel,
        grid=(t // tm, n // tn),
        in_specs=[
            pl.BlockSpec((tm, d), lambda m, j: (m, 0)),
            _mod_spec(d, layer, nb, tpb, shift_chunk),
            _mod_spec(d, layer, nb, tpb, shift_chunk + 1),
            pl.BlockSpec((d, tn), lambda m, j: (0, j)),
            pl.BlockSpec((1, tn), lambda m, j: (0, j)),
        ],
        out_specs=pl.BlockSpec((tm, tn), lambda m, j: (m, j)),
        out_shape=jax.ShapeDtypeStruct((t, n), BF16),
        scratch_shapes=[pltpu.VMEM((tm, d), BF16)],
        compiler_params=_params("parallel", "arbitrary"),
        name="modmm",
    )(x, modr, modr, w, col_scale)


def _modmm_swiglu_kernel(x_ref, sh_ref, sc_ref, wg_ref, wu_ref, o_ref, h_ref):
    _modulate_into(h_ref, x_ref, sh_ref, sc_ref)
    h = h_ref[...]
    g = jnp.dot(h, wg_ref[...], preferred_element_type=F32)
    u = jnp.dot(h, wu_ref[...], preferred_element_type=F32)
    o_ref[...] = (g * jax.nn.sigmoid(g) * u).astype(o_ref.dtype)


def modmm_swiglu(x, modr, layer, nb, shift_chunk, w_gu, tm=1024, tn=512):
    t, d = x.shape
    f = w_gu.shape[1] // 2
    tm, tn = _tile(t // nb, tm), _tile(f, tn)
    tpb = (t // nb) // tm
    nf = f // tn
    return pl.pallas_call(
        _modmm_swiglu_kernel,
        grid=(t // tm, nf),
        in_specs=[
            pl.BlockSpec((tm, d), lambda m, j: (m, 0)),
            _mod_spec(d, layer, nb, tpb, shift_chunk),
            _mod_spec(d, layer, nb, tpb, shift_chunk + 1),
            pl.BlockSpec((d, tn), lambda m, j: (0, j)),
            pl.BlockSpec((d, tn), lambda m, j: (0, j + nf)),
        ],
        out_specs=pl.BlockSpec((tm, tn), lambda m, j: (m, j)),
        out_shape=jax.ShapeDtypeStruct((t, f), BF16),
        scratch_shapes=[pltpu.VMEM((tm, d), BF16)],
        compiler_params=_params("parallel", "arbitrary"),
        name="modmm_swiglu",
    )(x, modr, modr, w_gu, w_gu)


def _res_ln(x, y, gate, ln_g, ln_b, alpha):
    v = alpha * x + (1.0 + gate) * y
    mu = jnp.mean(v, axis=-1, keepdims=True)
    dv = v - mu
    var = jnp.mean(dv * dv, axis=-1, keepdims=True)
    return dv * lax.rsqrt(var + LN_EPS) * ln_g + ln_b


def _mm_res_ln_kernel(a_ref, w_ref, x_ref, gate_ref, lng_ref, lnb_ref, o_ref, acc_ref, *, alpha, nk):
    k = pl.program_id(1)
    part = jnp.dot(a_ref[...], w_ref[...], preferred_element_type=F32)

    @pl.when(k == 0)
    def _():
        acc_ref[...] = part

    @pl.when(k > 0)
    def _():
        acc_ref[...] += part

    @pl.when(k == nk - 1)
    def _():
        o_ref[...] = _res_ln(x_ref[...], acc_ref[...], gate_ref[0], lng_ref[...], lnb_ref[...], alpha)


def mm_res_ln(a, w, x, modr, layer, nb, gate_chunk, ln_g, ln_b, alpha, tm=512, tk=2048):
    t, kdim = a.shape
    d = w.shape[1]
    tm, tk = _tile(t // nb, tm), _tile(kdim, tk)
    tpb = (t // nb) // tm
    nk = kdim // tk
    return pl.pallas_call(
        functools.partial(_mm_res_ln_kernel, alpha=alpha, nk=nk),
        grid=(t // tm, nk),
        in_specs=[
            pl.BlockSpec((tm, tk), lambda m, k: (m, k)),
            pl.BlockSpec((tk, d), lambda m, k: (k, 0)),
            pl.BlockSpec((tm, d), lambda m, k: (m, 0)),
            _mod_spec(d, layer, nb, tpb, gate_chunk),
            pl.BlockSpec((1, d), lambda m, k: (0, 0)),
            pl.BlockSpec((1, d), lambda m, k: (0, 0)),
        ],
        out_specs=pl.BlockSpec((tm, d), lambda m, k: (m, 0)),
        out_shape=jax.ShapeDtypeStruct((t, d), F32),
        scratch_shapes=[pltpu.VMEM((tm, d), F32)],
        compiler_params=_params("parallel", "arbitrary"),
        name="mm_res_ln",
    )(a, w, x, modr, ln_g.reshape(1, d), ln_b.reshape(1, d))


def _head_cols(h):
    return slice(h * HEAD_DIM, (h + 1) * HEAD_DIM)


def _sb_attn_kernel(q_ref, k_ref, v_ref, o_ref, *, tq, nq, heads):
    row = lax.broadcasted_iota(jnp.int32, (tq, tq), 0)
    col = lax.broadcasted_iota(jnp.int32, (tq, tq), 1)
    strict = col < row
    later = strict.astype(BF16)
    later2 = jnp.concatenate([later, later], axis=0)

    def blocks(qs, k0, carries, masked):
        hs = range(heads)
        zs = [lax.dot_general(qs[h], k_ref[pl.ds(k0, tq), _head_cols(h)], (((1,), (1,)), ((), ())),
                              preferred_element_type=F32) for h in hs]
        bases, splits, sums = [], [], []
        for z in zs:
            sp = jnp.maximum(z, 0.0) + jnp.log(1.0 + jnp.exp2(-jnp.abs(z))) * LOG2_E
            bases.append(z - sp)
            if masked:
                sp = jnp.where(strict, sp, 0.0)
            hi = sp.astype(BF16)
            lo = (sp - hi.astype(F32)).astype(BF16)
            splits.append(jnp.concatenate([hi, lo], axis=1))
            sums.append(jnp.sum(sp, axis=-1, keepdims=True))
        tails = [jnp.dot(split, later2, preferred_element_type=F32) for split in splits]
        probs = [jnp.exp2(bases[h] - tails[h] - carries[h][0]) for h in hs]
        if masked:
            probs = [jnp.where(strict, a, 0.0) for a in probs]
        accs = [carries[h][1] + jnp.dot(probs[h].astype(BF16), v_ref[pl.ds(k0, tq), _head_cols(h)],
                                        preferred_element_type=F32) for h in hs]
        runs = [carries[h][0] + sums[h] for h in hs]
        return tuple(zip(runs, accs))

    def qbody(qi, _):
        q0 = pl.multiple_of(qi * tq, tq)
        qs = [q_ref[pl.ds(q0, tq), _head_cols(h)] for h in range(heads)]
        carries = tuple((jnp.zeros((tq, 1), F32), jnp.zeros((tq, HEAD_DIM), F32)) for _ in range(heads))
        carries = blocks(qs, q0, carries, True)

        def kbody(i, carries):
            return blocks(qs, pl.multiple_of((qi - 1 - i) * tq, tq), carries, False)

        carries = lax.fori_loop(0, qi, kbody, carries)
        for h in range(heads):
            o_ref[pl.ds(q0, tq), _head_cols(h)] = carries[h][1].astype(o_ref.dtype)
        return 0

    lax.fori_loop(0, nq, qbody, 0)


def sb_attention(qkv, nb, tq=256, heads=4):
    t, n3 = qkv.shape
    s = t // nb
    nh = n3 // (3 * HEAD_DIM)
    tq = _tile(s, tq)
    heads = min(heads, nh)
    ng = nh // heads
    cols = heads * HEAD_DIM
    return pl.pallas_call(
        functools.partial(_sb_attn_kernel, tq=tq, nq=s // tq, heads=heads),
        grid=(nb, ng),
        in_specs=[
            pl.BlockSpec((s, cols), lambda b, g: (b, g)),
            pl.BlockSpec((s, cols), lambda b, g: (b, ng + g)),
            pl.BlockSpec((s, cols), lambda b, g: (b, 2 * ng + g)),
        ],
        out_specs=pl.BlockSpec((s, cols), lambda b, g: (b, g)),
        out_shape=jax.ShapeDtypeStruct((t, nh * HEAD_DIM), BF16),
        compiler_params=_params("parallel", "parallel"),
        name="sb_attention",
    )(qkv, qkv, qkv)


def _mla_in_kernel(x_ref, sh_ref, sc_ref, w_ref, qn_ref, kvn_ref, cos_ref, sin_ref,
                   cq_ref, ckv_ref, kr_ref, *, q_rank, kv_rank):
    h = (x_ref[...] * (1.0 + sc_ref[0]) + sh_ref[0]).astype(BF16)
    lat = jnp.dot(h, w_ref[...], preferred_element_type=F32)

    def rms(v, g):
        return v * lax.rsqrt(jnp.mean(v * v, axis=-1, keepdims=True) + RMS_EPS) * g

    cq_ref[...] = rms(lat[:, :q_rank], qn_ref[...]).astype(cq_ref.dtype)
    ckv_ref[...] = rms(lat[:, q_rank:q_rank + kv_rank], kvn_ref[...]).astype(ckv_ref.dtype)
    kr_ref[...] = _rope(lat[:, q_rank + kv_rank:], cos_ref[...], sin_ref[...]).astype(kr_ref.dtype)


def mla_in(x, modr, layer, nb, w_in_pad, q_norm, kv_norm, cos, sin, tm=512):
    t, d = x.shape
    q_rank, kv_rank = q_norm.shape[0], kv_norm.shape[0]
    n = w_in_pad.shape[1]
    assert n == q_rank + kv_rank + LANES
    tm = _tile(t // nb, tm)
    tpb = (t // nb) // tm
    return pl.pallas_call(
        functools.partial(_mla_in_kernel, q_rank=q_rank, kv_rank=kv_rank),
        grid=(t // tm,),
        in_specs=[
            pl.BlockSpec((tm, d), lambda m: (m, 0)),
            _mod_spec(d, layer, nb, tpb, 0),
            _mod_spec(d, layer, nb, tpb, 1),
            pl.BlockSpec((d, n), lambda m: (0, 0)),
            pl.BlockSpec((1, q_rank), lambda m: (0, 0)),
            pl.BlockSpec((1, kv_rank), lambda m: (0, 0)),
            pl.BlockSpec((tm, LANES), lambda m: (m, 0)),
            pl.BlockSpec((tm, LANES), lambda m: (m, 0)),
        ],
        out_specs=[
            pl.BlockSpec((tm, q_rank), lambda m: (m, 0)),
            pl.BlockSpec((tm, kv_rank), lambda m: (m, 0)),
            pl.BlockSpec((tm, LANES), lambda m: (m, 0)),
        ],
        out_shape=[
            jax.ShapeDtypeStruct((t, q_rank), BF16),
            jax.ShapeDtypeStruct((t, kv_rank), BF16),
            jax.ShapeDtypeStruct((t, LANES), BF16),
        ],
        compiler_params=_params("parallel"),
        name="mla_in",
    )(x, modr, modr, w_in_pad, q_norm.reshape(1, q_rank), kv_norm.reshape(1, kv_rank), cos, sin)


def _mm_kernel(a_ref, w_ref, o_ref):
    o_ref[...] = jnp.dot(a_ref[...], w_ref[...], preferred_element_type=F32).astype(o_ref.dtype)


def mm(a, w, tm=1024, tn=1024):
    t, k = a.shape
    n = w.shape[1]
    tm, tn = _tile(t, tm), _tile(n, tn)
    return pl.pallas_call(
        _mm_kernel,
        grid=(t // tm, n // tn),
        in_specs=[pl.BlockSpec((tm, k), lambda m, j: (m, 0)), pl.BlockSpec((k, tn), lambda m, j: (0, j))],
        out_specs=pl.BlockSpec((tm, tn), lambda m, j: (m, j)),
        out_shape=jax.ShapeDtypeStruct((t, n), BF16),
        compiler_params=_params("parallel", "parallel"),
        name="mm",
    )(a, w)


def _mm_q_rope_kernel(a_ref, w_ref, cos_ref, sin_ref, o_ref, *, scale, heads):
    acc = jnp.dot(a_ref[...], w_ref[...], preferred_element_type=F32)
    cos, sin = cos_ref[...], sin_ref[...]
    for h in range(heads):
        c0 = 2 * LANES * h
        o_ref[:, c0:c0 + LANES] = (acc[:, c0:c0 + LANES] * scale).astype(o_ref.dtype)
        roped = _rope(acc[:, c0 + LANES:c0 + 2 * LANES], cos, sin)
        o_ref[:, c0 + LANES:c0 + 2 * LANES] = (roped * scale).astype(o_ref.dtype)


def mm_q_rope(a, w, cos, sin, scale, tm=1024, heads_per_tile=4):
    t, k = a.shape
    n = w.shape[1]
    tm = _tile(t, tm)
    tn = _tile(n, 2 * LANES * heads_per_tile)
    return pl.pallas_call(
        functools.partial(_mm_q_rope_kernel, scale=scale, heads=tn // (2 * LANES)),
        grid=(t // tm, n // tn),
        in_specs=[
            pl.BlockSpec((tm, k), lambda m, j: (m, 0)),
            pl.BlockSpec((k, tn), lambda m, j: (0, j)),
            pl.BlockSpec((tm, LANES), lambda m, j: (m, 0)),
            pl.BlockSpec((tm, LANES), lambda m, j: (m, 0)),
        ],
        out_specs=pl.BlockSpec((tm, tn), lambda m, j: (m, j)),
        out_shape=jax.ShapeDtypeStruct((t, n), BF16),
        compiler_params=_params("parallel", "parallel"),
        name="mm_q_rope",
    )(a, w, cos, sin)


def _mm_t_kernel(a_ref, wt_ref, o_ref, *, bk):
    res = lax.dot_general(wt_ref[...], a_ref[...], (((1,), (1,)), ((), ())), preferred_element_type=F32)
    for j in range(o_ref.shape[0]):
        o_ref[j] = res[:, j * bk:(j + 1) * bk].astype(o_ref.dtype)


def mm_t(a, wt, bk, tm=1024, tn=1024):
    t, k = a.shape
    n = wt.shape[0]
    tm, tn = _tile(t, tm, bk), _tile(n, tn)
    return pl.pallas_call(
        functools.partial(_mm_t_kernel, bk=bk),
        grid=(t // tm, n // tn),
        in_specs=[pl.BlockSpec((tm, k), lambda m, j: (m, 0)), pl.BlockSpec((tn, k), lambda m, j: (j, 0))],
        out_specs=pl.BlockSpec((tm // bk, tn, bk), lambda m, j: (m, j, 0)),
        out_shape=jax.ShapeDtypeStruct((t // bk, n, bk), BF16),
        compiler_params=_params("parallel", "parallel"),
        name="mm_t",
    )(a, wt)


def _mla_attn_kernel(q_ref, kn_ref, kr_ref, vt_ref, o_ref, kcat_ref, *, tq, nq, heads):
    for h in range(heads):
        c0 = 2 * HEAD_DIM * h
        kcat_ref[:, c0:c0 + HEAD_DIM] = kn_ref[:, _head_cols(h)]
        kcat_ref[:, c0 + HEAD_DIM:c0 + 2 * HEAD_DIM] = kr_ref[...]
    key = lax.broadcasted_iota(jnp.int32, (tq, tq), 0)
    qry = lax.broadcasted_iota(jnp.int32, (tq, tq), 1)
    causal = key <= qry

    def blocks(qs, kb, carries, masked):
        hs = range(heads)
        k0 = pl.multiple_of(kb * tq, tq)
        ss = [lax.dot_general(kcat_ref[pl.ds(k0, tq), 2 * HEAD_DIM * h:2 * HEAD_DIM * (h + 1)], qs[h],
                              (((1,), (1,)), ((), ())), preferred_element_type=F32) for h in hs]
        if masked:
            ss = [jnp.where(causal, s, -jnp.inf) for s in ss]
        m_new, ps, corrs, ls = [], [], [], []
        for h in hs:
            m_old, l_old, _ = carries[h]
            m = jnp.maximum(m_old, jnp.max(ss[h], axis=0, keepdims=True))
            p = jnp.exp(ss[h] - m)
            corr = jnp.exp(m_old - m)
            m_new.append(m)
            corrs.append(corr)
            ls.append(l_old * corr + jnp.sum(p, axis=0, keepdims=True))
            ps.append(p.astype(BF16))
        accs = [carries[h][2] * corrs[h] + jnp.dot(vt_ref[kb, _head_cols(h), :], ps[h], preferred_element_type=F32)
                for h in hs]
        return tuple(zip(m_new, ls, accs))

    def qbody(qi, _):
        q0 = pl.multiple_of(qi * tq, tq)
        qs = [q_ref[pl.ds(q0, tq), 2 * HEAD_DIM * h:2 * HEAD_DIM * (h + 1)] for h in range(heads)]
        carries = tuple((jnp.full((1, tq), -jnp.inf, F32), jnp.zeros((1, tq), F32), jnp.zeros((HEAD_DIM, tq), F32))
                        for _ in range(heads))
        carries = blocks(qs, qi, carries, True)

        def kbody(i, carries):
            return blocks(qs, i, carries, False)

        carries = lax.fori_loop(0, qi, kbody, carries)
        for h in range(heads):
            _, l, acc = carries[h]
            o_ref[pl.ds(q0, tq), _head_cols(h)] = (acc / l).T.astype(o_ref.dtype)
        return 0

    lax.fori_loop(0, nq, qbody, 0)


def mla_attention(q, kn, kr, vt, nb, heads=4):
    t = q.shape[0]
    s = t // nb
    nh = q.shape[1] // (2 * HEAD_DIM)
    tq = vt.shape[2]
    heads = min(heads, nh)
    return pl.pallas_call(
        functools.partial(_mla_attn_kernel, tq=tq, nq=s // tq, heads=heads),
        grid=(nb, nh // heads),
        in_specs=[
            pl.BlockSpec((s, 2 * HEAD_DIM * heads), lambda b, g: (b, g)),
            pl.BlockSpec((s, HEAD_DIM * heads), lambda b, g: (b, g)),
            pl.BlockSpec((s, LANES), lambda b, g: (b, 0)),
            pl.BlockSpec((s // tq, HEAD_DIM * heads, tq), lambda b, g: (b, g, 0)),
        ],
        out_specs=pl.BlockSpec((s, HEAD_DIM * heads), lambda b, g: (b, g)),
        out_shape=jax.ShapeDtypeStruct((t, nh * HEAD_DIM), BF16),
        scratch_shapes=[pltpu.VMEM((s, 2 * HEAD_DIM * heads), BF16)],
        compiler_params=_params("parallel", "parallel"),
        name="mla_attention",
    )(q, kn, kr, vt)


def _router_kernel(x_ref, sh_ref, sc_ref, wr_ref, h_ref, idx_ref, gate_ref, cnt_ref, tri_ref, base_ref,
                   *, tm, n_exp):
    @pl.when(pl.program_id(0) == 0)
    def _():
        r = lax.broadcasted_iota(jnp.int32, (tm, tm), 0)
        c = lax.broadcasted_iota(jnp.int32, (tm, tm), 1)
        tri_ref[...] = (c <= r).astype(BF16)
        base_ref[...] = jnp.zeros_like(base_ref)

    h = x_ref[...] * (1.0 + sc_ref[0]) + sh_ref[0]
    ns = h.shape[1] // LANES
    for c in range(ns):
        h_ref[pl.ds(c, tm, stride=ns), :] = h[:, c * LANES:(c + 1) * LANES]
    h_hi = h.astype(BF16)
    h_lo = (h - h_hi.astype(F32)).astype(BF16)
    w = wr_ref[...]
    w_hi = w.astype(BF16)
    w_lo = (w - w_hi.astype(F32)).astype(BF16)
    logits = (jnp.dot(h_hi, w_hi, preferred_element_type=F32)
              + jnp.dot(h_lo, w_hi, preferred_element_type=F32)
              + jnp.dot(h_hi, w_lo, preferred_element_type=F32))
    lane = lax.broadcasted_iota(jnp.int32, logits.shape, 1).astype(F32)
    lg = jnp.where(lane < n_exp, logits, -jnp.inf)
    m1 = jnp.max(lg, axis=-1, keepdims=True)
    i1 = jnp.min(jnp.where(lg == m1, lane, float(LANES)), axis=-1, keepdims=True)
    lg2 = jnp.where(lane == i1, -jnp.inf, lg)
    m2 = jnp.max(lg2, axis=-1, keepdims=True)
    i2 = jnp.min(jnp.where(lg2 == m2, lane, float(LANES)), axis=-1, keepdims=True)
    e = jnp.exp(m2 - m1)
    g1 = 1.0 / (1.0 + e)
    g2 = e * g1
    oh1 = lane == i1
    oh2 = lane == i2
    onehot = jnp.where(oh1 | oh2, 1.0, 0.0).astype(BF16)
    tot = base_ref[...] + jnp.dot(tri_ref[...], onehot, preferred_element_type=F32)
    r1 = jnp.sum(jnp.where(oh1, tot, 0.0), axis=-1, keepdims=True) - 1.0
    r2 = jnp.sum(jnp.where(oh2, tot, 0.0), axis=-1, keepdims=True) - 1.0
    last = tot[tm - 1:tm, :]
    base_ref[...] = last
    cnt_ref[...] = jnp.broadcast_to(last, cnt_ref.shape)
    info = jnp.where(lane == 0, i1, jnp.where(lane == 1, i2, jnp.where(lane == 2, r1, jnp.where(lane == 3, r2, 0.0))))
    idx_ref[...] = info.astype(jnp.int32)
    gate_ref[...] = jnp.where(lane == 0, g1, jnp.where(lane == 1, g2, 0.0))


def router(x, modr, layer, nb, w_router, tm=512):
    t, d = x.shape
    n_exp = w_router.shape[1]
    tm = _tile(t // nb, tm)
    tpb = (t // nb) // tm
    ns = d // LANES
    wr = jnp.pad(w_router, ((0, 0), (0, LANES - n_exp)))
    return pl.pallas_call(
        functools.partial(_router_kernel, tm=tm, n_exp=n_exp),
        grid=(t // tm,),
        in_specs=[
            pl.BlockSpec((tm, d), lambda m: (m, 0)),
            _mod_spec(d, layer, nb, tpb, 3),
            _mod_spec(d, layer, nb, tpb, 4),
            pl.BlockSpec((d, LANES), lambda m: (0, 0)),
        ],
        out_specs=[
            pl.BlockSpec((tm * ns, LANES), lambda m: (m, 0)),
            pl.BlockSpec((tm, LANES), lambda m: (m, 0)),
            pl.BlockSpec((tm, LANES), lambda m: (m, 0)),
            pl.BlockSpec((8, LANES), lambda m: (0, 0)),
        ],
        out_shape=[
            jax.ShapeDtypeStruct((t * ns, LANES), F32),
            jax.ShapeDtypeStruct((t, LANES), jnp.int32),
            jax.ShapeDtypeStruct((t, LANES), F32),
            jax.ShapeDtypeStruct((8, LANES), F32),
        ],
        scratch_shapes=[pltpu.VMEM((tm, tm), BF16), pltpu.VMEM((1, LANES), F32)],
        compiler_params=_params("arbitrary"),
        name="router",
    )(x, modr, modr, wr)


ROW_DMA_UNROLL = 8


def _slab(ref, row, ns):
    return ref.at[pl.ds(pl.multiple_of(row * ns, ns), ns), :]


def _gather_rows_kernel(idx_ref, idx_next_ref, src_ref, o_ref, buf_ref, sem, *, tg, ns, steps):
    i = pl.program_id(0)
    slot = i % 2

    def issue(ids_ref, s):
        def start(r, _):
            pltpu.make_async_copy(_slab(src_ref, ids_ref[0, 0, r], ns), _slab(buf_ref, s * tg + r, ns),
                                  sem.at[s]).start()
            return 0
        lax.fori_loop(0, tg, start, 0, unroll=ROW_DMA_UNROLL)

    @pl.when(i == 0)
    def _():
        issue(idx_ref, 0)

    @pl.when(i + 1 < steps)
    def _():
        issue(idx_next_ref, 1 - slot)

    base = pl.multiple_of(slot * (tg * ns), tg * ns)
    pltpu.make_async_copy(src_ref.at[pl.ds(0, tg * ns), :], buf_ref.at[pl.ds(base, tg * ns), :], sem.at[slot]).wait()
    for c in range(ns):
        o_ref[:, c * LANES:(c + 1) * LANES] = buf_ref[pl.ds(base + c, tg, stride=ns), :].astype(o_ref.dtype)


def gather_rows(src, idx, ns, tg=512):
    n = idx.shape[0]
    tg = _tile(n, tg)
    steps = n // tg
    ids = idx.reshape(steps, 1, tg)
    return pl.pallas_call(
        functools.partial(_gather_rows_kernel, tg=tg, ns=ns, steps=steps),
        grid=(steps,),
        in_specs=[
            pl.BlockSpec((1, 1, tg), lambda m: (m, 0, 0), memory_space=pltpu.SMEM),
            pl.BlockSpec((1, 1, tg), lambda m: (jnp.minimum(m + 1, steps - 1), 0, 0), memory_space=pltpu.SMEM),
            pl.BlockSpec(memory_space=pl.ANY),
        ],
        out_specs=pl.BlockSpec((tg, ns * LANES), lambda m: (m, 0)),
        out_shape=jax.ShapeDtypeStruct((n, ns * LANES), BF16),
        scratch_shapes=[pltpu.VMEM((2 * tg * ns, LANES), src.dtype), pltpu.SemaphoreType.DMA((2,))],
        compiler_params=_params("arbitrary"),
        name="gather_rows",
    )(ids, ids, src)


def _gmm_swiglu_kernel(te_ref, nu_ref, a_ref, wg_ref, wu_ref, o_ref, wgb_ref, wub_ref):
    m = pl.program_id(1)
    used = m < nu_ref[0]
    fresh = jnp.logical_or(m == 0, te_ref[m] != te_ref[jnp.maximum(m - 1, 0)])

    @pl.when(fresh)
    def _():
        wgb_ref[...] = wg_ref[...].astype(BF16)
        wub_ref[...] = wu_ref[...].astype(BF16)

    @pl.when(used)
    def _():
        a = a_ref[...]
        g = jnp.dot(a, wgb_ref[...], preferred_element_type=F32)
        u = jnp.dot(a, wub_ref[...], preferred_element_type=F32)
        o_ref[...] = (g * jax.nn.sigmoid(g) * u).astype(o_ref.dtype)

    @pl.when(jnp.logical_not(used))
    def _():
        o_ref[...] = jnp.zeros_like(o_ref)


def gmm_swiglu(a, w_gu, layer, tile_expert, n_used, tm, tn=1024):
    ms, d = a.shape
    f = w_gu.shape[3] // 2
    tn = _tile(f, tn)
    nf = f // tn
    return pl.pallas_call(
        _gmm_swiglu_kernel,
        grid_spec=pltpu.PrefetchScalarGridSpec(
            num_scalar_prefetch=2,
            grid=(nf, ms // tm),
            in_specs=[
                pl.BlockSpec((tm, d), lambda j, m, te, nu: (m, 0)),
                pl.BlockSpec((None, None, d, tn), lambda j, m, te, nu: (layer, te[m], 0, j)),
                pl.BlockSpec((None, None, d, tn), lambda j, m, te, nu: (layer, te[m], 0, j + nf)),
            ],
            out_specs=pl.BlockSpec((tm, tn), lambda j, m, te, nu: (m, j)),
            scratch_shapes=[pltpu.VMEM((d, tn), BF16), pltpu.VMEM((d, tn), BF16)],
        ),
        out_shape=jax.ShapeDtypeStruct((ms, f), BF16),
        compiler_params=_params("parallel", "arbitrary"),
        name="gmm_swiglu",
    )(tile_expert, n_used, a, w_gu, w_gu)


def _gmm_kernel(te_ref, nu_ref, a_ref, w_ref, o_ref):
    used = pl.program_id(1) < nu_ref[0]

    @pl.when(used)
    def _():
        acc = jnp.dot(a_ref[...], w_ref[...], preferred_element_type=F32)
        for c in range(o_ref.shape[1]):
            o_ref[:, c, :] = acc[:, c * LANES:(c + 1) * LANES]

    @pl.when(jnp.logical_not(used))
    def _():
        o_ref[...] = jnp.zeros_like(o_ref)


def gmm(a, w, layer, tile_expert, n_used, tm, tn=1024):
    ms, k = a.shape
    n = w.shape[3]
    tn = _tile(n, tn, 8 * LANES)
    return pl.pallas_call(
        _gmm_kernel,
        grid_spec=pltpu.PrefetchScalarGridSpec(
            num_scalar_prefetch=2,
            grid=(n // tn, ms // tm),
            in_specs=[
                pl.BlockSpec((tm, k), lambda j, m, te, nu: (m, 0)),
                pl.BlockSpec((None, None, k, tn), lambda j, m, te, nu: (layer, te[m], 0, j)),
            ],
            out_specs=pl.BlockSpec((tm, tn // LANES, LANES), lambda j, m, te, nu: (m, j, 0)),
        ),
        out_shape=jax.ShapeDtypeStruct((ms, n // LANES, LANES), F32),
        compiler_params=_params("parallel", "arbitrary"),
        name="gmm",
    )(tile_expert, n_used, a, w)


def _combine_ln_kernel(p0_ref, p1_ref, p0_next_ref, p1_next_ref, y_ref, gate_ref, x_ref, gmod_ref, lng_ref, lnb_ref,
                       o_ref, ybuf_ref, ys_ref, sem, *, tm, ns, steps, alpha):
    i = pl.program_id(0)
    slot = i % 2

    def issue(pa_ref, pb_ref, s):
        def start(r, _):
            row = (s * TOP_K) * tm + r
            pltpu.make_async_copy(_slab(y_ref, pa_ref[0, 0, r], ns), _slab(ybuf_ref, row, ns), sem.at[s]).start()
            pltpu.make_async_copy(_slab(y_ref, pb_ref[0, 0, r], ns), _slab(ybuf_ref, row + tm, ns), sem.at[s]).start()
            return 0
        lax.fori_loop(0, tm, start, 0, unroll=ROW_DMA_UNROLL)

    @pl.when(i == 0)
    def _():
        issue(p0_ref, p1_ref, 0)

    @pl.when(i + 1 < steps)
    def _():
        issue(p0_next_ref, p1_next_ref, 1 - slot)

    rows = TOP_K * tm * ns
    base = pl.multiple_of(slot * rows, rows)
    pltpu.make_async_copy(y_ref.at[pl.ds(0, rows), :], ybuf_ref.at[pl.ds(base, rows), :], sem.at[slot]).wait()
    gates = gate_ref[...]
    g0, g1 = gates[:, 0:1], gates[:, 1:2]
    for c in range(ns):
        y0 = ybuf_ref[pl.ds(base + c, tm, stride=ns), :]
        y1 = ybuf_ref[pl.ds(base + tm * ns + c, tm, stride=ns), :]
        ys_ref[:, c * LANES:(c + 1) * LANES] = g0 * y0 + g1 * y1
    o_ref[...] = _res_ln(x_ref[...], ys_ref[...], gmod_ref[0], lng_ref[...], lnb_ref[...], alpha)


def combine_ln(y_slabs, pos0, pos1, gates, x, modr, layer, nb, ln_g, ln_b, alpha, tm=256):
    t, d = x.shape
    tm = _tile(t // nb, tm)
    tpb = (t // nb) // tm
    nt = t // tm
    ns = d // LANES
    ids_spec = pl.BlockSpec((1, 1, tm), lambda m: (m, 0, 0), memory_space=pltpu.SMEM)
    ids_next_spec = pl.BlockSpec((1, 1, tm), lambda m: (jnp.minimum(m + 1, nt - 1), 0, 0), memory_space=pltpu.SMEM)
    p0, p1 = pos0.reshape(nt, 1, tm), pos1.reshape(nt, 1, tm)
    return pl.pallas_call(
        functools.partial(_combine_ln_kernel, tm=tm, ns=ns, steps=nt, alpha=alpha),
        grid=(nt,),
        in_specs=[
            ids_spec,
            ids_spec,
            ids_next_spec,
            ids_next_spec,
            pl.BlockSpec(memory_space=pl.ANY),
            pl.BlockSpec((tm, LANES), lambda m: (m, 0)),
            pl.BlockSpec((tm, d), lambda m: (m, 0)),
            _mod_spec(d, layer, nb, tpb, 5),
            pl.BlockSpec((1, d), lambda m: (0, 0)),
            pl.BlockSpec((1, d), lambda m: (0, 0)),
        ],
        out_specs=pl.BlockSpec((tm, d), lambda m: (m, 0)),
        out_shape=jax.ShapeDtypeStruct((t, d), F32),
        scratch_shapes=[pltpu.VMEM((2 * TOP_K * tm * ns, LANES), F32), pltpu.VMEM((tm, d), F32),
                        pltpu.SemaphoreType.DMA((2,))],
        compiler_params=_params("arbitrary"),
        name="combine_ln",
    )(p0, p1, p0, p1, y_slabs.reshape(-1, LANES), gates, x, modr, ln_g.reshape(1, d), ln_b.reshape(1, d))


def moe_layer(x, modr, layer, nb, w_router, w_gu, w_down, moe_index, ln_g, ln_b, alpha, tm_g=512):
    t, d = x.shape
    n_exp = w_router.shape[1]
    h, info, gates, cnt = router(x, modr, layer, nb, w_router)
    counts = cnt[0, :n_exp].astype(jnp.int32)
    sizes = (counts + tm_g - 1) // tm_g * tm_g
    ends = jnp.cumsum(sizes)
    starts = ends - sizes
    pos0 = starts[info[:, 0]] + info[:, 2]
    pos1 = starts[info[:, 1]] + info[:, 3]
    ms = t * TOP_K + n_exp * tm_g
    tok = jnp.arange(t, dtype=jnp.int32)
    slot_token = jnp.zeros((ms,), jnp.int32).at[pos0].set(tok).at[pos1].set(tok)
    tile_start = jnp.arange(ms // tm_g, dtype=jnp.int32) * tm_g
    tile_expert = jnp.minimum(jnp.sum(tile_start[:, None] >= ends[None, :], axis=1), n_exp - 1).astype(jnp.int32)
    n_used = (ends[-1:] // tm_g).astype(jnp.int32)

    h_sorted = gather_rows(h, slot_token, d // LANES)
    act = gmm_swiglu(h_sorted, w_gu, moe_index, tile_expert, n_used, tm_g)
    y_slabs = gmm(act, w_down, moe_index, tile_expert, n_used, tm_g)
    return combine_ln(y_slabs, pos0, pos1, gates, x, modr, layer, nb, ln_g, ln_b, alpha)


def kernel(x, c, positions, mod_w, mod_b, ln_g, ln_b, sb_w_qkv, sb_w_o, mla_w_in, mla_q_norm, mla_kv_norm,
           mla_w_uq, mla_w_ukv, mla_w_o, ffn_w_gu, ffn_w_down, moe_router, moe_w_gu, moe_w_down):
    nb, s, d = x.shape
    depth = mod_w.shape[0]
    t = nb * s
    alpha = float((2 * depth) ** 0.25)
    xf = x.reshape(t, d)

    modr = mod_table(c, mod_w, mod_b)
    cos, sin = rope_tables(positions)

    sb_heads = sb_w_o.shape[1] // HEAD_DIM
    sb_scale = HEAD_DIM ** -0.5 * LOG2_E
    sb_col_scale = jnp.concatenate([jnp.full((sb_heads * HEAD_DIM,), sb_scale, F32),
                                    jnp.ones((2 * sb_heads * HEAD_DIM,), F32)]).reshape(1, -1)

    mla_heads = mla_w_o.shape[1] // HEAD_DIM
    q_rank = mla_q_norm.shape[1]
    mla_scale = float((HEAD_DIM + ROPE_DIM) ** -0.5)
    moe_w_down_bf16 = moe_w_down.astype(BF16)

    for i in range(depth):
        j = i // 2
        if i % 2 == 0:
            qkv = modmm(xf, modr, i, nb, 0, sb_w_qkv[j].astype(BF16), sb_col_scale)
            o = sb_attention(qkv, nb)
            w_o = sb_w_o[j]
        else:
            w_in = jnp.pad(mla_w_in[j], ((0, 0), (0, LANES - ROPE_DIM))).astype(BF16)
            cq, ckv, kr = mla_in(xf, modr, i, nb, w_in, mla_q_norm[j], mla_kv_norm[j], cos, sin)
            w_uq = mla_w_uq[j].reshape(q_rank, mla_heads, HEAD_DIM + ROPE_DIM)
            w_uq = jnp.pad(w_uq, ((0, 0), (0, 0), (0, LANES - ROPE_DIM))).reshape(q_rank, mla_heads * 2 * LANES)
            q = mm_q_rope(cq, w_uq.astype(BF16), cos, sin, mla_scale)
            w_ukv = mla_w_ukv[j].reshape(-1, mla_heads, 2 * HEAD_DIM)
            w_uk = w_ukv[:, :, :HEAD_DIM].reshape(-1, mla_heads * HEAD_DIM).astype(BF16)
            w_uv_t = w_ukv[:, :, HEAD_DIM:].reshape(-1, mla_heads * HEAD_DIM).T.astype(BF16)
            kn = mm(ckv, w_uk)
            vt = mm_t(ckv, w_uv_t, _tile(s, ATTN_BLOCK))
            o = mla_attention(q, kn, kr, vt, nb)
            w_o = mla_w_o[j]
        xf = mm_res_ln(o, w_o.astype(BF16), xf, modr, i, nb, 2, ln_g[i, 0], ln_b[i, 0], alpha)
        if i % 2 == 0:
            act = modmm_swiglu(xf, modr, i, nb, 3, ffn_w_gu[j].astype(BF16))
            xf = mm_res_ln(act, ffn_w_down[j].astype(BF16), xf, modr, i, nb, 5, ln_g[i, 1], ln_b[i, 1], alpha)
        else:
            xf = moe_layer(xf, modr, i, nb, moe_router[j], moe_w_gu, moe_w_down_bf16, j,
                           ln_g[i, 1], ln_b[i, 1], alpha)
    return xf.reshape(nb, s, d)
```

```python
import functools

import jax
import jax.numpy as jnp
from jax import lax
from jax.experimental import pallas as pl
from jax.experimental.pallas import tpu as pltpu

F32 = jnp.float32
BF16 = jnp.bfloat16

HEAD_DIM = 128
ROPE_DIM = 64
ROPE_HALF = ROPE_DIM // 2
ROPE_THETA = 10000.0
N_MOD = 6
TOP_K = 2
LN_EPS = 1e-5
RMS_EPS = 1e-6
LOG2_E = 1.4426950408889634
ATTN_BLOCK = 512
SB_KEY_BLOCK = 256
LANES = 128
VMEM_LIMIT = 56 * 1024 * 1024


def _tile(n, want, align=LANES):
    if n <= want:
        return n
    t = want - want % align
    while n % t:
        t -= align
    return t


def _params(*sem):
    return pltpu.CompilerParams(dimension_semantics=sem, vmem_limit_bytes=VMEM_LIMIT)


def _mod_kernel(c_ref, w_ref, b_ref, o_ref):
    c = c_ref[...]
    cond = (c * jax.nn.sigmoid(c)).astype(BF16)
    o_ref[...] = jnp.dot(cond, w_ref[...].astype(BF16), preferred_element_type=F32) + b_ref[...]


def mod_table(c, mod_w, mod_b):
    depth, d, n = mod_w.shape
    b = c.shape[0]
    tn = _tile(n, 1024)
    out = pl.pallas_call(
        _mod_kernel,
        grid=(depth, n // tn),
        in_specs=[
            pl.BlockSpec((b, d), lambda i, j: (0, 0)),
            pl.BlockSpec((None, d, tn), lambda i, j: (i, 0, j)),
            pl.BlockSpec((None, 1, tn), lambda i, j: (i, 0, j)),
        ],
        out_specs=pl.BlockSpec((None, b, tn), lambda i, j: (i, 0, j)),
        out_shape=jax.ShapeDtypeStruct((depth, b, n), F32),
        compiler_params=_params("parallel", "parallel"),
        name="mod_table",
    )(c, mod_w, mod_b.reshape(depth, 1, n))
    return out.reshape(depth * b * N_MOD, 1, d)


def _mod_spec(d, layer, nb, tiles_per_batch, chunk):
    def index(m, *_):
        return ((layer * nb + m // tiles_per_batch) * N_MOD + chunk, 0, 0)
    return pl.BlockSpec((1, 1, d), index)


def _rope_table_kernel(pos_ref, invf_ref, cos_ref, sin_ref):
    ang = pos_ref[...].astype(F32) * invf_ref[...]
    lane = lax.broadcasted_iota(jnp.int32, ang.shape, 1)
    live = lane < ROPE_DIM
    cos_ref[...] = jnp.where(live, jnp.cos(ang), 0.0)
    sin_ref[...] = jnp.where(live, jnp.sin(ang), 0.0)


def rope_tables(positions):
    t = positions.size
    tm = _tile(t, 1024)
    inv_freq = ROPE_THETA ** (-jnp.arange(ROPE_HALF, dtype=F32) / ROPE_HALF)
    invf = jnp.concatenate([inv_freq, inv_freq, jnp.zeros((LANES - ROPE_DIM,), F32)]).reshape(1, LANES)
    return pl.pallas_call(
        _rope_table_kernel,
        grid=(t // tm,),
        in_specs=[pl.BlockSpec((tm, 1), lambda m: (m, 0)), pl.BlockSpec((1, LANES), lambda m: (0, 0))],
        out_specs=[pl.BlockSpec((tm, LANES), lambda m: (m, 0))] * 2,
        out_shape=[jax.ShapeDtypeStruct((t, LANES), F32)] * 2,
        compiler_params=_params("parallel"),
        name="rope_tables",
    )(positions.reshape(t, 1), invf)


def _rope(x, cos, sin):
    lane = lax.broadcasted_iota(jnp.int32, x.shape, 1)
    rot = jnp.where(lane < ROPE_HALF, -pltpu.roll(x, LANES - ROPE_HALF, 1), pltpu.roll(x, ROPE_HALF, 1))
    return x * cos + rot * sin


def _modulate_into(h_ref, x_ref, sh_ref, sc_ref):
    @pl.when(pl.program_id(1) == 0)
    def _():
        h_ref[...] = (x_ref[...] * (1.0 + sc_ref[0]) + sh_ref[0]).astype(h_ref.dtype)


def _modmm_kernel(x_ref, sh_ref, sc_ref, w_ref, cs_ref, o_ref, h_ref):
    _modulate_into(h_ref, x_ref, sh_ref, sc_ref)
    acc = jnp.dot(h_ref[...], w_ref[...], preferred_element_type=F32)
    o_ref[...] = (acc * cs_ref[...]).astype(o_ref.dtype)


def modmm(x, modr, layer, nb, shift_chunk, w, col_scale, tm=1024, tn=1024):
    t, d = x.shape
    n = w.shape[1]
    tm, tn = _tile(t // nb, tm), _tile(n, tn)
    tpb = (t // nb) // tm
    return pl.pallas_call(
        _modmm_kernel,
        grid=(t // tm, n // tn),
        in_specs=[
            pl.BlockSpec((tm, d), lambda m, j: (m, 0)),
            _mod_spec(d, layer, nb, tpb, shift_chunk),
            _mod_spec(d, layer, nb, tpb, shift_chunk + 1),
            pl.BlockSpec((d, tn), lambda m, j: (0, j)),
            pl.BlockSpec((1, tn), lambda m, j: (0, j)),
        ],
        out_specs=pl.BlockSpec((tm, tn), lambda m, j: (m, j)),
        out_shape=jax.ShapeDtypeStruct((t, n), BF16),
        scratch_shapes=[pltpu.VMEM((tm, d), BF16)],
        compiler_params=_params("parallel", "arbitrary"),
        name="modmm",
    )(x, modr, modr, w, col_scale)


def _modmm_t_kernel(x_ref, sh_ref, sc_ref, wt_ref, o_ref, h_ref, *, bk):
    _modulate_into(h_ref, x_ref, sh_ref, sc_ref)
    res = lax.dot_general(wt_ref[...], h_ref[...], (((1,), (1,)), ((), ())), preferred_element_type=F32)
    for j in range(o_ref.shape[0]):
        o_ref[j] = res[:, j * bk:(j + 1) * bk].astype(o_ref.dtype)


def modmm_t(x, modr, layer, nb, shift_chunk, wt, bk, tm=1024, tn=1024):
    t, d = x.shape
    n = wt.shape[0]
    tm, tn = _tile(t // nb, tm, bk), _tile(n, tn)
    tpb = (t // nb) // tm
    return pl.pallas_call(
        functools.partial(_modmm_t_kernel, bk=bk),
        grid=(t // tm, n // tn),
        in_specs=[
            pl.BlockSpec((tm, d), lambda m, j: (m, 0)),
            _mod_spec(d, layer, nb, tpb, shift_chunk),
            _mod_spec(d, layer, nb, tpb, shift_chunk + 1),
            pl.BlockSpec((tn, d), lambda m, j: (j, 0)),
        ],
        out_specs=pl.BlockSpec((tm // bk, tn, bk), lambda m, j: (m, j, 0)),
        out_shape=jax.ShapeDtypeStruct((t // bk, n, bk), BF16),
        scratch_shapes=[pltpu.VMEM((tm, d), BF16)],
        compiler_params=_params("parallel", "arbitrary"),
        name="modmm_t",
    )(x, modr, modr, wt)


def _modmm_swiglu_kernel(x_ref, sh_ref, sc_ref, wg_ref, wu_ref, o_ref, h_ref):
    _modulate_into(h_ref, x_ref, sh_ref, sc_ref)
    h = h_ref[...]
    g = jnp.dot(h, wg_ref[...], preferred_element_type=F32)
    u = jnp.dot(h, wu_ref[...], preferred_element_type=F32)
    o_ref[...] = (g * jax.nn.sigmoid(g) * u).astype(o_ref.dtype)


def modmm_swiglu(x, modr, layer, nb, shift_chunk, w_gu, tm=1024, tn=512):
    t, d = x.shape
    f = w_gu.shape[1] // 2
    tm, tn = _tile(t // nb, tm), _tile(f, tn)
    tpb = (t // nb) // tm
    nf = f // tn
    return pl.pallas_call(
        _modmm_swiglu_kernel,
        grid=(t // tm, nf),
        in_specs=[
            pl.BlockSpec((tm, d), lambda m, j: (m, 0)),
            _mod_spec(d, layer, nb, tpb, shift_chunk),
            _mod_spec(d, layer, nb, tpb, shift_chunk + 1),
            pl.BlockSpec((d, tn), lambda m, j: (0, j)),
            pl.BlockSpec((d, tn), lambda m, j: (0, j + nf)),
        ],
        out_specs=pl.BlockSpec((tm, tn), lambda m, j: (m, j)),
        out_shape=jax.ShapeDtypeStruct((t, f), BF16),
        scratch_shapes=[pltpu.VMEM((tm, d), BF16)],
        compiler_params=_params("parallel", "arbitrary"),
        name="modmm_swiglu",
    )(x, modr, modr, w_gu, w_gu)


def _res_ln(x, y, gate, ln_g, ln_b, alpha):
    v = alpha * x + (1.0 + gate) * y
    mu = jnp.mean(v, axis=-1, keepdims=True)
    dv = v - mu
    var = jnp.mean(dv * dv, axis=-1, keepdims=True)
    return dv * lax.rsqrt(var + LN_EPS) * ln_g + ln_b


def _mm_res_ln_kernel(a_ref, w_ref, x_ref, gate_ref, lng_ref, lnb_ref, o_ref, acc_ref, *, alpha, nk):
    k = pl.program_id(1)
    part = jnp.dot(a_ref[...], w_ref[...], preferred_element_type=F32)

    @pl.when(k == 0)
    def _():
        acc_ref[...] = part

    @pl.when(k > 0)
    def _():
        acc_ref[...] += part

    @pl.when(k == nk - 1)
    def _():
        o_ref[...] = _res_ln(x_ref[...], acc_ref[...], gate_ref[0], lng_ref[...], lnb_ref[...], alpha)


def mm_res_ln(a, w, x, modr, layer, nb, gate_chunk, ln_g, ln_b, alpha, tm=512, tk=2048):
    t, kdim = a.shape
    d = w.shape[1]
    tm, tk = _tile(t // nb, tm), _tile(kdim, tk)
    tpb = (t // nb) // tm
    nk = kdim // tk
    return pl.pallas_call(
        functools.partial(_mm_res_ln_kernel, alpha=alpha, nk=nk),
        grid=(t // tm, nk),
        in_specs=[
            pl.BlockSpec((tm, tk), lambda m, k: (m, k)),
            pl.BlockSpec((tk, d), lambda m, k: (k, 0)),
            pl.BlockSpec((tm, d), lambda m, k: (m, 0)),
            _mod_spec(d, layer, nb, tpb, gate_chunk),
            pl.BlockSpec((1, d), lambda m, k: (0, 0)),
            pl.BlockSpec((1, d), lambda m, k: (0, 0)),
        ],
        out_specs=pl.BlockSpec((tm, d), lambda m, k: (m, 0)),
        out_shape=jax.ShapeDtypeStruct((t, d), F32),
        scratch_shapes=[pltpu.VMEM((tm, d), F32)],
        compiler_params=_params("parallel", "arbitrary"),
        name="mm_res_ln",
    )(a, w, x, modr, ln_g.reshape(1, d), ln_b.reshape(1, d))


def _head_cols(h):
    return slice(h * HEAD_DIM, (h + 1) * HEAD_DIM)


def _sb_attn_kernel(q_ref, k_ref, vt_ref, o_ref, *, bk, nq, heads):
    tq = 2 * bk
    key = lax.broadcasted_iota(jnp.int32, (bk, tq), 0)
    qry = lax.broadcasted_iota(jnp.int32, (bk, tq), 1)
    r2 = lax.broadcasted_iota(jnp.int32, (bk, bk), 0)
    c2 = lax.broadcasted_iota(jnp.int32, (bk, bk), 1)
    later = (c2 > r2).astype(BF16)
    later2 = jnp.concatenate([later, later], axis=1)

    def blocks(qs, kbs, offsets, carries):
        hs = range(heads)
        units = [(h, i) for i in range(len(kbs)) for h in hs]
        masks = [None if off is None else key + off < qry for off in offsets]
        zs = [lax.dot_general(k_ref[pl.ds(pl.multiple_of(kbs[i] * bk, bk), bk), _head_cols(h)], qs[h],
                              (((1,), (1,)), ((), ())), preferred_element_type=F32) for h, i in units]
        bases, splits, sums = [], [], []
        for (h, i), z in zip(units, zs):
            neg_abs = pltpu.bitcast(pltpu.bitcast(z, jnp.uint32) | jnp.uint32(0x80000000), F32)
            sp = jnp.maximum(z, 0.0) + jnp.log(1.0 + jnp.exp2(neg_abs)) * LOG2_E
            bases.append(z - sp)
            if masks[i] is not None:
                sp = jnp.where(masks[i], sp, 0.0)
            hi = sp.astype(BF16)
            lo = (sp - hi.astype(F32)).astype(BF16)
            splits.append(jnp.concatenate([hi, lo], axis=0))
            sums.append(jnp.sum(sp, axis=0, keepdims=True))
        tails = [jnp.dot(later2, split, preferred_element_type=F32) for split in splits]
        runs = [carries[h][0] for h in hs]
        probs = []
        for u, (h, i) in enumerate(units):
            a = jnp.exp2(bases[u] - tails[u] - runs[h])
            if masks[i] is not None:
                a = jnp.where(masks[i], a, 0.0)
            probs.append(a.astype(BF16))
            runs[h] = runs[h] + sums[u]
        accs = [carries[h][1] for h in hs]
        for u, (h, i) in enumerate(units):
            accs[h] = accs[h] + jnp.dot(vt_ref[kbs[i], _head_cols(h), :], probs[u], preferred_element_type=F32)
        return tuple(zip(runs, accs))

    def qbody(qi, _):
        q0 = pl.multiple_of(qi * tq, tq)
        qs = [q_ref[pl.ds(q0, tq), _head_cols(h)] for h in range(heads)]
        carries = tuple((jnp.zeros((1, tq), F32), jnp.zeros((HEAD_DIM, tq), F32)) for _ in range(heads))
        carries = blocks(qs, [2 * qi + 1, 2 * qi], [bk, 0], carries)

        def kbody(i, carries):
            pair = qi - 1 - i
            return blocks(qs, [2 * pair + 1, 2 * pair], [None, None], carries)

        carries = lax.fori_loop(0, qi, kbody, carries)
        for h in range(heads):
            o_ref[pl.ds(q0, tq), _head_cols(h)] = carries[h][1].T.astype(o_ref.dtype)
        return 0

    lax.fori_loop(0, nq, qbody, 0)


def sb_attention(qk, vt, nb, heads=4):
    t, n2 = qk.shape
    s = t // nb
    nh = n2 // (2 * HEAD_DIM)
    bk = vt.shape[2]
    assert s % (2 * bk) == 0
    heads = min(heads, nh)
    ng = nh // heads
    cols = heads * HEAD_DIM
    return pl.pallas_call(
        functools.partial(_sb_attn_kernel, bk=bk, nq=s // (2 * bk), heads=heads),
        grid=(nb, ng),
        in_specs=[
            pl.BlockSpec((s, cols), lambda b, g: (b, g)),
            pl.BlockSpec((s, cols), lambda b, g: (b, ng + g)),
            pl.BlockSpec((s // bk, cols, bk), lambda b, g: (b, g, 0)),
        ],
        out_specs=pl.BlockSpec((s, cols), lambda b, g: (b, g)),
        out_shape=jax.ShapeDtypeStruct((t, nh * HEAD_DIM), BF16),
        compiler_params=_params("parallel", "parallel"),
        name="sb_attention",
    )(qk, qk, vt)


def _mla_in_kernel(x_ref, sh_ref, sc_ref, w_ref, qn_ref, kvn_ref, cos_ref, sin_ref,
                   cq_ref, ckv_ref, kr_ref, *, q_rank, kv_rank):
    h = (x_ref[...] * (1.0 + sc_ref[0]) + sh_ref[0]).astype(BF16)
    lat = jnp.dot(h, w_ref[...], preferred_element_type=F32)

    def rms(v, g):
        return v * lax.rsqrt(jnp.mean(v * v, axis=-1, keepdims=True) + RMS_EPS) * g

    cq_ref[...] = rms(lat[:, :q_rank], qn_ref[...]).astype(cq_ref.dtype)
    ckv_ref[...] = rms(lat[:, q_rank:q_rank + kv_rank], kvn_ref[...]).astype(ckv_ref.dtype)
    kr_ref[...] = _rope(lat[:, q_rank + kv_rank:], cos_ref[...], sin_ref[...]).astype(kr_ref.dtype)


def mla_in(x, modr, layer, nb, w_in_pad, q_norm, kv_norm, cos, sin, tm=512):
    t, d = x.shape
    q_rank, kv_rank = q_norm.shape[0], kv_norm.shape[0]
    n = w_in_pad.shape[1]
    assert n == q_rank + kv_rank + LANES
    tm = _tile(t // nb, tm)
    tpb = (t // nb) // tm
    return pl.pallas_call(
        functools.partial(_mla_in_kernel, q_rank=q_rank, kv_rank=kv_rank),
        grid=(t // tm,),
        in_specs=[
            pl.BlockSpec((tm, d), lambda m: (m, 0)),
            _mod_spec(d, layer, nb, tpb, 0),
            _mod_spec(d, layer, nb, tpb, 1),
            pl.BlockSpec((d, n), lambda m: (0, 0)),
            pl.BlockSpec((1, q_rank), lambda m: (0, 0)),
            pl.BlockSpec((1, kv_rank), lambda m: (0, 0)),
            pl.BlockSpec((tm, LANES), lambda m: (m, 0)),
            pl.BlockSpec((tm, LANES), lambda m: (m, 0)),
        ],
        out_specs=[
            pl.BlockSpec((tm, q_rank), lambda m: (m, 0)),
            pl.BlockSpec((tm, kv_rank), lambda m: (m, 0)),
            pl.BlockSpec((tm, LANES), lambda m: (m, 0)),
        ],
        out_shape=[
            jax.ShapeDtypeStruct((t, q_rank), BF16),
            jax.ShapeDtypeStruct((t, kv_rank), BF16),
            jax.ShapeDtypeStruct((t, LANES), BF16),
        ],
        compiler_params=_params("parallel"),
        name="mla_in",
    )(x, modr, modr, w_in_pad, q_norm.reshape(1, q_rank), kv_norm.reshape(1, kv_rank), cos, sin)


def _mm_kernel(a_ref, w_ref, o_ref):
    o_ref[...] = jnp.dot(a_ref[...], w_ref[...], preferred_element_type=F32).astype(o_ref.dtype)


def mm(a, w, tm=1024, tn=1024):
    t, k = a.shape
    n = w.shape[1]
    tm, tn = _tile(t, tm), _tile(n, tn)
    return pl.pallas_call(
        _mm_kernel,
        grid=(t // tm, n // tn),
        in_specs=[pl.BlockSpec((tm, k), lambda m, j: (m, 0)), pl.BlockSpec((k, tn), lambda m, j: (0, j))],
        out_specs=pl.BlockSpec((tm, tn), lambda m, j: (m, j)),
        out_shape=jax.ShapeDtypeStruct((t, n), BF16),
        compiler_params=_params("parallel", "parallel"),
        name="mm",
    )(a, w)


def _mm_q_rope_kernel(a_ref, w_ref, cos_ref, sin_ref, o_ref, *, scale, heads):
    acc = jnp.dot(a_ref[...], w_ref[...], preferred_element_type=F32)
    cos, sin = cos_ref[...], sin_ref[...]
    for h in range(heads):
        c0 = 2 * LANES * h
        o_ref[:, c0:c0 + LANES] = (acc[:, c0:c0 + LANES] * scale).astype(o_ref.dtype)
        roped = _rope(acc[:, c0 + LANES:c0 + 2 * LANES], cos, sin)
        o_ref[:, c0 + LANES:c0 + 2 * LANES] = (roped * scale).astype(o_ref.dtype)


def mm_q_rope(a, w, cos, sin, scale, tm=1024, heads_per_tile=4):
    t, k = a.shape
    n = w.shape[1]
    tm = _tile(t, tm)
    tn = _tile(n, 2 * LANES * heads_per_tile)
    return pl.pallas_call(
        functools.partial(_mm_q_rope_kernel, scale=scale, heads=tn // (2 * LANES)),
        grid=(t // tm, n // tn),
        in_specs=[
            pl.BlockSpec((tm, k), lambda m, j: (m, 0)),
            pl.BlockSpec((k, tn), lambda m, j: (0, j)),
            pl.BlockSpec((tm, LANES), lambda m, j: (m, 0)),
            pl.BlockSpec((tm, LANES), lambda m, j: (m, 0)),
        ],
        out_specs=pl.BlockSpec((tm, tn), lambda m, j: (m, j)),
        out_shape=jax.ShapeDtypeStruct((t, n), BF16),
        compiler_params=_params("parallel", "parallel"),
        name="mm_q_rope",
    )(a, w, cos, sin)


def _mm_t_kernel(a_ref, wt_ref, o_ref, *, bk):
    res = lax.dot_general(wt_ref[...], a_ref[...], (((1,), (1,)), ((), ())), preferred_element_type=F32)
    for j in range(o_ref.shape[0]):
        o_ref[j] = res[:, j * bk:(j + 1) * bk].astype(o_ref.dtype)


def mm_t(a, wt, bk, tm=1024, tn=1024):
    t, k = a.shape
    n = wt.shape[0]
    tm, tn = _tile(t, tm, bk), _tile(n, tn)
    return pl.pallas_call(
        functools.partial(_mm_t_kernel, bk=bk),
        grid=(t // tm, n // tn),
        in_specs=[pl.BlockSpec((tm, k), lambda m, j: (m, 0)), pl.BlockSpec((tn, k), lambda m, j: (j, 0))],
        out_specs=pl.BlockSpec((tm // bk, tn, bk), lambda m, j: (m, j, 0)),
        out_shape=jax.ShapeDtypeStruct((t // bk, n, bk), BF16),
        compiler_params=_params("parallel", "parallel"),
        name="mm_t",
    )(a, wt)


def _mla_attn_kernel(q_ref, kn_ref, kr_ref, vt_ref, o_ref, kcat_ref, *, tq, nq, heads):
    for h in range(heads):
        c0 = 2 * HEAD_DIM * h
        kcat_ref[:, c0:c0 + HEAD_DIM] = kn_ref[:, _head_cols(h)]
        kcat_ref[:, c0 + HEAD_DIM:c0 + 2 * HEAD_DIM] = kr_ref[...]
    key = lax.broadcasted_iota(jnp.int32, (tq, tq), 0)
    qry = lax.broadcasted_iota(jnp.int32, (tq, tq), 1)
    causal = key <= qry

    def blocks(qs, kb, carries, masked):
        hs = range(heads)
        k0 = pl.multiple_of(kb * tq, tq)
        ss = [lax.dot_general(kcat_ref[pl.ds(k0, tq), 2 * HEAD_DIM * h:2 * HEAD_DIM * (h + 1)], qs[h],
                              (((1,), (1,)), ((), ())), preferred_element_type=F32) for h in hs]
        if masked:
            ss = [jnp.where(causal, s, -jnp.inf) for s in ss]
        m_new, ps, corrs, ls = [], [], [], []
        for h in hs:
            m_old, l_old, _ = carries[h]
            m = jnp.maximum(m_old, jnp.max(ss[h], axis=0, keepdims=True))
            p = jnp.exp(ss[h] - m)
            corr = jnp.exp(m_old - m)
            m_new.append(m)
            corrs.append(corr)
            ls.append(l_old * corr + jnp.sum(p, axis=0, keepdims=True))
            ps.append(p.astype(BF16))
        accs = [carries[h][2] * corrs[h] + jnp.dot(vt_ref[kb, _head_cols(h), :], ps[h], preferred_element_type=F32)
                for h in hs]
        return tuple(zip(m_new, ls, accs))

    def qbody(qi, _):
        q0 = pl.multiple_of(qi * tq, tq)
        qs = [q_ref[pl.ds(q0, tq), 2 * HEAD_DIM * h:2 * HEAD_DIM * (h + 1)] for h in range(heads)]
        carries = tuple((jnp.full((1, tq), -jnp.inf, F32), jnp.zeros((1, tq), F32), jnp.zeros((HEAD_DIM, tq), F32))
                        for _ in range(heads))
        carries = blocks(qs, qi, carries, True)

        def kbody(i, carries):
            return blocks(qs, i, carries, False)

        carries = lax.fori_loop(0, qi, kbody, carries)
        for h in range(heads):
            _, l, acc = carries[h]
            o_ref[pl.ds(q0, tq), _head_cols(h)] = (acc / l).T.astype(o_ref.dtype)
        return 0

    lax.fori_loop(0, nq, qbody, 0)


def mla_attention(q, kn, kr, vt, nb, heads=4):
    t = q.shape[0]
    s = t // nb
    nh = q.shape[1] // (2 * HEAD_DIM)
    tq = vt.shape[2]
    heads = min(heads, nh)
    return pl.pallas_call(
        functools.partial(_mla_attn_kernel, tq=tq, nq=s // tq, heads=heads),
        grid=(nb, nh // heads),
        in_specs=[
            pl.BlockSpec((s, 2 * HEAD_DIM * heads), lambda b, g: (b, g)),
            pl.BlockSpec((s, HEAD_DIM * heads), lambda b, g: (b, g)),
            pl.BlockSpec((s, LANES), lambda b, g: (b, 0)),
            pl.BlockSpec((s // tq, HEAD_DIM * heads, tq), lambda b, g: (b, g, 0)),
        ],
        out_specs=pl.BlockSpec((s, HEAD_DIM * heads), lambda b, g: (b, g)),
        out_shape=jax.ShapeDtypeStruct((t, nh * HEAD_DIM), BF16),
        scratch_shapes=[pltpu.VMEM((s, 2 * HEAD_DIM * heads), BF16)],
        compiler_params=_params("parallel", "parallel"),
        name="mla_attention",
    )(q, kn, kr, vt)


def _router_kernel(x_ref, sh_ref, sc_ref, wr_ref, h_ref, idx_ref, gate_ref, cnt_ref, tri_ref, base_ref,
                   *, tm, n_exp):
    @pl.when(pl.program_id(0) == 0)
    def _():
        r = lax.broadcasted_iota(jnp.int32, (tm, tm), 0)
        c = lax.broadcasted_iota(jnp.int32, (tm, tm), 1)
        tri_ref[...] = (c <= r).astype(BF16)
        base_ref[...] = jnp.zeros_like(base_ref)

    h = x_ref[...] * (1.0 + sc_ref[0]) + sh_ref[0]
    ns = h.shape[1] // LANES
    for c in range(ns):
        h_ref[pl.ds(c, tm, stride=ns), :] = h[:, c * LANES:(c + 1) * LANES]
    h_hi = h.astype(BF16)
    h_lo = (h - h_hi.astype(F32)).astype(BF16)
    w = wr_ref[...]
    w_hi = w.astype(BF16)
    w_lo = (w - w_hi.astype(F32)).astype(BF16)
    logits = (jnp.dot(h_hi, w_hi, preferred_element_type=F32)
              + jnp.dot(h_lo, w_hi, preferred_element_type=F32)
              + jnp.dot(h_hi, w_lo, preferred_element_type=F32))
    lane = lax.broadcasted_iota(jnp.int32, logits.shape, 1).astype(F32)
    lg = jnp.where(lane < n_exp, logits, -jnp.inf)
    m1 = jnp.max(lg, axis=-1, keepdims=True)
    i1 = jnp.min(jnp.where(lg == m1, lane, float(LANES)), axis=-1, keepdims=True)
    lg2 = jnp.where(lane == i1, -jnp.inf, lg)
    m2 = jnp.max(lg2, axis=-1, keepdims=True)
    i2 = jnp.min(jnp.where(lg2 == m2, lane, float(LANES)), axis=-1, keepdims=True)
    e = jnp.exp(m2 - m1)
    g1 = 1.0 / (1.0 + e)
    g2 = e * g1
    oh1 = lane == i1
    oh2 = lane == i2
    onehot = jnp.where(oh1 | oh2, 1.0, 0.0).astype(BF16)
    tot = base_ref[...] + jnp.dot(tri_ref[...], onehot, preferred_element_type=F32)
    r1 = jnp.sum(jnp.where(oh1, tot, 0.0), axis=-1, keepdims=True) - 1.0
    r2 = jnp.sum(jnp.where(oh2, tot, 0.0), axis=-1, keepdims=True) - 1.0
    last = tot[tm - 1:tm, :]
    base_ref[...] = last
    cnt_ref[...] = jnp.broadcast_to(last, cnt_ref.shape)
    info = jnp.where(lane == 0, i1, jnp.where(lane == 1, i2, jnp.where(lane == 2, r1, jnp.where(lane == 3, r2, 0.0))))
    idx_ref[...] = info.astype(jnp.int32)
    gate_ref[...] = jnp.where(lane == 0, g1, jnp.where(lane == 1, g2, 0.0))


def router(x, modr, layer, nb, w_router, tm=512):
    t, d = x.shape
    n_exp = w_router.shape[1]
    tm = _tile(t // nb, tm)
    tpb = (t // nb) // tm
    ns = d // LANES
    wr = jnp.pad(w_router, ((0, 0), (0, LANES - n_exp)))
    return pl.pallas_call(
        functools.partial(_router_kernel, tm=tm, n_exp=n_exp),
        grid=(t // tm,),
        in_specs=[
            pl.BlockSpec((tm, d), lambda m: (m, 0)),
            _mod_spec(d, layer, nb, tpb, 3),
            _mod_spec(d, layer, nb, tpb, 4),
            pl.BlockSpec((d, LANES), lambda m: (0, 0)),
        ],
        out_specs=[
            pl.BlockSpec((tm * ns, LANES), lambda m: (m, 0)),
            pl.BlockSpec((tm, LANES), lambda m: (m, 0)),
            pl.BlockSpec((tm, LANES), lambda m: (m, 0)),
            pl.BlockSpec((8, LANES), lambda m: (0, 0)),
        ],
        out_shape=[
            jax.ShapeDtypeStruct((t * ns, LANES), F32),
            jax.ShapeDtypeStruct((t, LANES), jnp.int32),
            jax.ShapeDtypeStruct((t, LANES), F32),
            jax.ShapeDtypeStruct((8, LANES), F32),
        ],
        scratch_shapes=[pltpu.VMEM((tm, tm), BF16), pltpu.VMEM((1, LANES), F32)],
        compiler_params=_params("arbitrary"),
        name="router",
    )(x, modr, modr, wr)


ROW_DMA_UNROLL = 8


def _slab(ref, row, ns):
    return ref.at[pl.ds(pl.multiple_of(row * ns, ns), ns), :]


def _gather_rows_kernel(idx_ref, idx_next_ref, src_ref, o_ref, buf_ref, sem, *, tg, ns, steps):
    i = pl.program_id(0)
    slot = i % 2

    def issue(ids_ref, s):
        def start(r, _):
            pltpu.make_async_copy(_slab(src_ref, ids_ref[0, 0, r], ns), _slab(buf_ref, s * tg + r, ns),
                                  sem.at[s]).start()
            return 0
        lax.fori_loop(0, tg, start, 0, unroll=ROW_DMA_UNROLL)

    @pl.when(i == 0)
    def _():
        issue(idx_ref, 0)

    @pl.when(i + 1 < steps)
    def _():
        issue(idx_next_ref, 1 - slot)

    base = pl.multiple_of(slot * (tg * ns), tg * ns)
    pltpu.make_async_copy(src_ref.at[pl.ds(0, tg * ns), :], buf_ref.at[pl.ds(base, tg * ns), :], sem.at[slot]).wait()
    for c in range(ns):
        o_ref[:, c * LANES:(c + 1) * LANES] = buf_ref[pl.ds(base + c, tg, stride=ns), :].astype(o_ref.dtype)


def gather_rows(src, idx, ns, tg=512):
    n = idx.shape[0]
    tg = _tile(n, tg)
    steps = n // tg
    ids = idx.reshape(steps, 1, tg)
    return pl.pallas_call(
        functools.partial(_gather_rows_kernel, tg=tg, ns=ns, steps=steps),
        grid=(steps,),
        in_specs=[
            pl.BlockSpec((1, 1, tg), lambda m: (m, 0, 0), memory_space=pltpu.SMEM),
            pl.BlockSpec((1, 1, tg), lambda m: (jnp.minimum(m + 1, steps - 1), 0, 0), memory_space=pltpu.SMEM),
            pl.BlockSpec(memory_space=pl.ANY),
        ],
        out_specs=pl.BlockSpec((tg, ns * LANES), lambda m: (m, 0)),
        out_shape=jax.ShapeDtypeStruct((n, ns * LANES), BF16),
        scratch_shapes=[pltpu.VMEM((2 * tg * ns, LANES), src.dtype), pltpu.SemaphoreType.DMA((2,))],
        compiler_params=_params("arbitrary"),
        name="gather_rows",
    )(ids, ids, src)


def _gmm_swiglu_kernel(te_ref, nu_ref, a_ref, wg_ref, wu_ref, o_ref, wgb_ref, wub_ref):
    m = pl.program_id(1)
    used = m < nu_ref[0]
    fresh = jnp.logical_or(m == 0, te_ref[m] != te_ref[jnp.maximum(m - 1, 0)])

    @pl.when(fresh)
    def _():
        wgb_ref[...] = wg_ref[...].astype(BF16)
        wub_ref[...] = wu_ref[...].astype(BF16)

    @pl.when(used)
    def _():
        a = a_ref[...]
        g = jnp.dot(a, wgb_ref[...], preferred_element_type=F32)
        u = jnp.dot(a, wub_ref[...], preferred_element_type=F32)
        o_ref[...] = (g * jax.nn.sigmoid(g) * u).astype(o_ref.dtype)

    @pl.when(jnp.logical_not(used))
    def _():
        o_ref[...] = jnp.zeros_like(o_ref)


def gmm_swiglu(a, w_gu, layer, tile_expert, n_used, tm, tn=1024):
    ms, d = a.shape
    f = w_gu.shape[3] // 2
    tn = _tile(f, tn)
    nf = f // tn
    return pl.pallas_call(
        _gmm_swiglu_kernel,
        grid_spec=pltpu.PrefetchScalarGridSpec(
            num_scalar_prefetch=2,
            grid=(nf, ms // tm),
            in_specs=[
                pl.BlockSpec((tm, d), lambda j, m, te, nu: (m, 0)),
                pl.BlockSpec((None, None, d, tn), lambda j, m, te, nu: (layer, te[m], 0, j)),
                pl.BlockSpec((None, None, d, tn), lambda j, m, te, nu: (layer, te[m], 0, j + nf)),
            ],
            out_specs=pl.BlockSpec((tm, tn), lambda j, m, te, nu: (m, j)),
            scratch_shapes=[pltpu.VMEM((d, tn), BF16), pltpu.VMEM((d, tn), BF16)],
        ),
        out_shape=jax.ShapeDtypeStruct((ms, f), BF16),
        compiler_params=_params("parallel", "arbitrary"),
        name="gmm_swiglu",
    )(tile_expert, n_used, a, w_gu, w_gu)


def _gmm_kernel(te_ref, nu_ref, a_ref, w_ref, o_ref):
    used = pl.program_id(1) < nu_ref[0]

    @pl.when(used)
    def _():
        acc = jnp.dot(a_ref[...], w_ref[...], preferred_element_type=F32)
        for c in range(o_ref.shape[1]):
            o_ref[:, c, :] = acc[:, c * LANES:(c + 1) * LANES]

    @pl.when(jnp.logical_not(used))
    def _():
        o_ref[...] = jnp.zeros_like(o_ref)


def gmm(a, w, layer, tile_expert, n_used, tm, tn=1024):
    ms, k = a.shape
    n = w.shape[3]
    tn = _tile(n, tn, 8 * LANES)
    return pl.pallas_call(
        _gmm_kernel,
        grid_spec=pltpu.PrefetchScalarGridSpec(
            num_scalar_prefetch=2,
            grid=(n // tn, ms // tm),
            in_specs=[
                pl.BlockSpec((tm, k), lambda j, m, te, nu: (m, 0)),
                pl.BlockSpec((None, None, k, tn), lambda j, m, te, nu: (layer, te[m], 0, j)),
            ],
            out_specs=pl.BlockSpec((tm, tn // LANES, LANES), lambda j, m, te, nu: (m, j, 0)),
        ),
        out_shape=jax.ShapeDtypeStruct((ms, n // LANES, LANES), F32),
        compiler_params=_params("parallel", "arbitrary"),
        name="gmm",
    )(tile_expert, n_used, a, w)


def _combine_ln_kernel(p0_ref, p1_ref, p0_next_ref, p1_next_ref, y_ref, gate_ref, x_ref, gmod_ref, lng_ref, lnb_ref,
                       o_ref, ybuf_ref, ys_ref, sem, *, tm, ns, steps, alpha):
    i = pl.program_id(0)
    slot = i % 2

    def issue(pa_ref, pb_ref, s):
        def start(r, _):
            row = (s * TOP_K) * tm + r
            pltpu.make_async_copy(_slab(y_ref, pa_ref[0, 0, r], ns), _slab(ybuf_ref, row, ns), sem.at[s]).start()
            pltpu.make_async_copy(_slab(y_ref, pb_ref[0, 0, r], ns), _slab(ybuf_ref, row + tm, ns), sem.at[s]).start()
            return 0
        lax.fori_loop(0, tm, start, 0, unroll=ROW_DMA_UNROLL)

    @pl.when(i == 0)
    def _():
        issue(p0_ref, p1_ref, 0)

    @pl.when(i + 1 < steps)
    def _():
        issue(p0_next_ref, p1_next_ref, 1 - slot)

    rows = TOP_K * tm * ns
    base = pl.multiple_of(slot * rows, rows)
    pltpu.make_async_copy(y_ref.at[pl.ds(0, rows), :], ybuf_ref.at[pl.ds(base, rows), :], sem.at[slot]).wait()
    gates = gate_ref[...]
    g0, g1 = gates[:, 0:1], gates[:, 1:2]
    for c in range(ns):
        y0 = ybuf_ref[pl.ds(base + c, tm, stride=ns), :]
        y1 = ybuf_ref[pl.ds(base + tm * ns + c, tm, stride=ns), :]
        ys_ref[:, c * LANES:(c + 1) * LANES] = g0 * y0 + g1 * y1
    o_ref[...] = _res_ln(x_ref[...], ys_ref[...], gmod_ref[0], lng_ref[...], lnb_ref[...], alpha)


def combine_ln(y_slabs, pos0, pos1, gates, x, modr, layer, nb, ln_g, ln_b, alpha, tm=256):
    t, d = x.shape
    tm = _tile(t // nb, tm)
    tpb = (t // nb) // tm
    nt = t // tm
    ns = d // LANES
    ids_spec = pl.BlockSpec((1, 1, tm), lambda m: (m, 0, 0), memory_space=pltpu.SMEM)
    ids_next_spec = pl.BlockSpec((1, 1, tm), lambda m: (jnp.minimum(m + 1, nt - 1), 0, 0), memory_space=pltpu.SMEM)
    p0, p1 = pos0.reshape(nt, 1, tm), pos1.reshape(nt, 1, tm)
    return pl.pallas_call(
        functools.partial(_combine_ln_kernel, tm=tm, ns=ns, steps=nt, alpha=alpha),
        grid=(nt,),
        in_specs=[
            ids_spec,
            ids_spec,
            ids_next_spec,
            ids_next_spec,
            pl.BlockSpec(memory_space=pl.ANY),
            pl.BlockSpec((tm, LANES), lambda m: (m, 0)),
            pl.BlockSpec((tm, d), lambda m: (m, 0)),
            _mod_spec(d, layer, nb, tpb, 5),
            pl.BlockSpec((1, d), lambda m: (0, 0)),
            pl.BlockSpec((1, d), lambda m: (0, 0)),
        ],
        out_specs=pl.BlockSpec((tm, d), lambda m: (m, 0)),
        out_shape=jax.ShapeDtypeStruct((t, d), F32),
        scratch_shapes=[pltpu.VMEM((2 * TOP_K * tm * ns, LANES), F32), pltpu.VMEM((tm, d), F32),
                        pltpu.SemaphoreType.DMA((2,))],
        compiler_params=_params("arbitrary"),
        name="combine_ln",
    )(p0, p1, p0, p1, y_slabs.reshape(-1, LANES), gates, x, modr, ln_g.reshape(1, d), ln_b.reshape(1, d))


def moe_layer(x, modr, layer, nb, w_router, w_gu, w_down, moe_index, ln_g, ln_b, alpha, tm_g=512):
    t, d = x.shape
    n_exp = w_router.shape[1]
    h, info, gates, cnt = router(x, modr, layer, nb, w_router)
    counts = cnt[0, :n_exp].astype(jnp.int32)
    sizes = (counts + tm_g - 1) // tm_g * tm_g
    ends = jnp.cumsum(sizes)
    starts = ends - sizes
    pos0 = starts[info[:, 0]] + info[:, 2]
    pos1 = starts[info[:, 1]] + info[:, 3]
    ms = t * TOP_K + n_exp * tm_g
    tok = jnp.arange(t, dtype=jnp.int32)
    slot_token = jnp.zeros((ms,), jnp.int32).at[jnp.concatenate([pos0, pos1])].set(jnp.concatenate([tok, tok]))
    tile_start = jnp.arange(ms // tm_g, dtype=jnp.int32) * tm_g
    tile_expert = jnp.minimum(jnp.sum(tile_start[:, None] >= ends[None, :], axis=1), n_exp - 1).astype(jnp.int32)
    n_used = (ends[-1:] // tm_g).astype(jnp.int32)

    h_sorted = gather_rows(h, slot_token, d // LANES)
    act = gmm_swiglu(h_sorted, w_gu, moe_index, tile_expert, n_used, tm_g)
    y_slabs = gmm(act, w_down, moe_index, tile_expert, n_used, tm_g)
    return combine_ln(y_slabs, pos0, pos1, gates, x, modr, layer, nb, ln_g, ln_b, alpha)


def kernel(x, c, positions, mod_w, mod_b, ln_g, ln_b, sb_w_qkv, sb_w_o, mla_w_in, mla_q_norm, mla_kv_norm,
           mla_w_uq, mla_w_ukv, mla_w_o, ffn_w_gu, ffn_w_down, moe_router, moe_w_gu, moe_w_down):
    nb, s, d = x.shape
    depth = mod_w.shape[0]
    t = nb * s
    alpha = float((2 * depth) ** 0.25)
    xf = x.reshape(t, d)

    modr = mod_table(c, mod_w, mod_b)
    cos, sin = rope_tables(positions)

    sb_heads = sb_w_o.shape[1] // HEAD_DIM
    sb_scale = HEAD_DIM ** -0.5 * LOG2_E
    sb_col_scale = jnp.concatenate([jnp.full((sb_heads * HEAD_DIM,), sb_scale, F32),
                                    jnp.ones((2 * sb_heads * HEAD_DIM,), F32)]).reshape(1, -1)

    mla_heads = mla_w_o.shape[1] // HEAD_DIM
    q_rank = mla_q_norm.shape[1]
    mla_scale = float((HEAD_DIM + ROPE_DIM) ** -0.5)
    moe_w_down_bf16 = moe_w_down.astype(BF16)

    for i in range(depth):
        j = i // 2
        if i % 2 == 0:
            n_qk = 2 * sb_heads * HEAD_DIM
            qk = modmm(xf, modr, i, nb, 0, sb_w_qkv[j, :, :n_qk].astype(BF16), sb_col_scale[:, :n_qk])
            vt = modmm_t(xf, modr, i, nb, 0, sb_w_qkv[j, :, n_qk:].T.astype(BF16), _tile(s, SB_KEY_BLOCK))
            o = sb_attention(qk, vt, nb)
            w_o = sb_w_o[j]
        else:
            w_in = jnp.pad(mla_w_in[j], ((0, 0), (0, LANES - ROPE_DIM))).astype(BF16)
            cq, ckv, kr = mla_in(xf, modr, i, nb, w_in, mla_q_norm[j], mla_kv_norm[j], cos, sin)
            w_uq = mla_w_uq[j].reshape(q_rank, mla_heads, HEAD_DIM + ROPE_DIM)
            w_uq = jnp.pad(w_uq, ((0, 0), (0, 0), (0, LANES - ROPE_DIM))).reshape(q_rank, mla_heads * 2 * LANES)
            q = mm_q_rope(cq, w_uq.astype(BF16), cos, sin, mla_scale)
            w_ukv = mla_w_ukv[j].reshape(-1, mla_heads, 2 * HEAD_DIM)
            w_uk = w_ukv[:, :, :HEAD_DIM].reshape(-1, mla_heads * HEAD_DIM).astype(BF16)
            w_uv_t = w_ukv[:, :, HEAD_DIM:].reshape(-1, mla_heads * HEAD_DIM).T.astype(BF16)
            kn = mm(ckv, w_uk)
            vt = mm_t(ckv, w_uv_t, _tile(s, ATTN_BLOCK))
            o = mla_attention(q, kn, kr, vt, nb)
            w_o = mla_w_o[j]
        xf = mm_res_ln(o, w_o.astype(BF16), xf, modr, i, nb, 2, ln_g[i, 0], ln_b[i, 0], alpha)
        if i % 2 == 0:
            act = modmm_swiglu(xf, modr, i, nb, 3, ffn_w_gu[j].astype(BF16))
            xf = mm_res_ln(act, ffn_w_down[j].astype(BF16), xf, modr, i, nb, 5, ln_g[i, 1], ln_b[i, 1], alpha)
        else:
            xf = moe_layer(xf, modr, i, nb, moe_router[j], moe_w_gu, moe_w_down_bf16, j,
                           ln_g[i, 1], ln_b[i, 1], alpha)
    return xf.reshape(nb, s, d)
```

```python
import functools

import jax
import jax.numpy as jnp
from jax import lax
from jax.experimental import pallas as pl
from jax.experimental.pallas import tpu as pltpu

F32 = jnp.float32
BF16 = jnp.bfloat16

HEAD_DIM = 128
ROPE_DIM = 64
ROPE_HALF = ROPE_DIM // 2
ROPE_THETA = 10000.0
N_MOD = 6
TOP_K = 2
LN_EPS = 1e-5
RMS_EPS = 1e-6
LOG2_E = 1.4426950408889634
ATTN_BLOCK = 512
SB_KEY_BLOCK = 256
LANES = 128
VMEM_LIMIT = 56 * 1024 * 1024


def _tile(n, want, align=LANES):
    if n <= want:
        return n
    t = want - want % align
    while n % t:
        t -= align
    return t


def _params(*sem):
    return pltpu.CompilerParams(dimension_semantics=sem, vmem_limit_bytes=VMEM_LIMIT)


def _mod_kernel(c_ref, w_ref, b_ref, o_ref):
    c = c_ref[...]
    cond = (c * jax.nn.sigmoid(c)).astype(BF16)
    o_ref[...] = jnp.dot(cond, w_ref[...].astype(BF16), preferred_element_type=F32) + b_ref[...]


def mod_table(c, mod_w, mod_b):
    depth, d, n = mod_w.shape
    b = c.shape[0]
    tn = _tile(n, 1024)
    out = pl.pallas_call(
        _mod_kernel,
        grid=(depth, n // tn),
        in_specs=[
            pl.BlockSpec((b, d), lambda i, j: (0, 0)),
            pl.BlockSpec((None, d, tn), lambda i, j: (i, 0, j)),
            pl.BlockSpec((None, 1, tn), lambda i, j: (i, 0, j)),
        ],
        out_specs=pl.BlockSpec((None, b, tn), lambda i, j: (i, 0, j)),
        out_shape=jax.ShapeDtypeStruct((depth, b, n), F32),
        compiler_params=_params("parallel", "parallel"),
        name="mod_table",
    )(c, mod_w, mod_b.reshape(depth, 1, n))
    return out.reshape(depth * b * N_MOD, 1, d)


def _mod_spec(d, layer, nb, tiles_per_batch, chunk):
    def index(m, *_):
        return ((layer * nb + m // tiles_per_batch) * N_MOD + chunk, 0, 0)
    return pl.BlockSpec((1, 1, d), index)


def _rope_table_kernel(pos_ref, invf_ref, cos_ref, sin_ref):
    ang = pos_ref[...].astype(F32) * invf_ref[...]
    lane = lax.broadcasted_iota(jnp.int32, ang.shape, 1)
    live = lane < ROPE_DIM
    cos_ref[...] = jnp.where(live, jnp.cos(ang), 0.0)
    sin_ref[...] = jnp.where(live, jnp.sin(ang), 0.0)


def rope_tables(positions):
    t = positions.size
    tm = _tile(t, 1024)
    inv_freq = ROPE_THETA ** (-jnp.arange(ROPE_HALF, dtype=F32) / ROPE_HALF)
    invf = jnp.concatenate([inv_freq, inv_freq, jnp.zeros((LANES - ROPE_DIM,), F32)]).reshape(1, LANES)
    return pl.pallas_call(
        _rope_table_kernel,
        grid=(t // tm,),
        in_specs=[pl.BlockSpec((tm, 1), lambda m: (m, 0)), pl.BlockSpec((1, LANES), lambda m: (0, 0))],
        out_specs=[pl.BlockSpec((tm, LANES), lambda m: (m, 0))] * 2,
        out_shape=[jax.ShapeDtypeStruct((t, LANES), F32)] * 2,
        compiler_params=_params("parallel"),
        name="rope_tables",
    )(positions.reshape(t, 1), invf)


def _rope(x, cos, sin):
    lane = lax.broadcasted_iota(jnp.int32, x.shape, 1)
    rot = jnp.where(lane < ROPE_HALF, -pltpu.roll(x, LANES - ROPE_HALF, 1), pltpu.roll(x, ROPE_HALF, 1))
    return x * cos + rot * sin


def _modulate_into(h_ref, x_ref, sh_ref, sc_ref):
    @pl.when(pl.program_id(1) == 0)
    def _():
        h_ref[...] = (x_ref[...] * (1.0 + sc_ref[0]) + sh_ref[0]).astype(h_ref.dtype)


def _modmm_kernel(x_ref, sh_ref, sc_ref, w_ref, cs_ref, o_ref, h_ref):
    _modulate_into(h_ref, x_ref, sh_ref, sc_ref)
    acc = jnp.dot(h_ref[...], w_ref[...], preferred_element_type=F32)
    o_ref[...] = (acc * cs_ref[...]).astype(o_ref.dtype)


def modmm(x, modr, layer, nb, shift_chunk, w, col_scale, tm=1024, tn=1024):
    t, d = x.shape
    n = w.shape[1]
    tm, tn = _tile(t // nb, tm), _tile(n, tn)
    tpb = (t // nb) // tm
    return pl.pallas_call(
        _modmm_kernel,
        grid=(t // tm, n // tn),
        in_specs=[
            pl.BlockSpec((tm, d), lambda m, j: (m, 0)),
            _mod_spec(d, layer, nb, tpb, shift_chunk),
            _mod_spec(d, layer, nb, tpb, shift_chunk + 1),
            pl.BlockSpec((d, tn), lambda m, j: (0, j)),
            pl.BlockSpec((1, tn), lambda m, j: (0, j)),
        ],
        out_specs=pl.BlockSpec((tm, tn), lambda m, j: (m, j)),
        out_shape=jax.ShapeDtypeStruct((t, n), BF16),
        scratch_shapes=[pltpu.VMEM((tm, d), BF16)],
        compiler_params=_params("parallel", "arbitrary"),
        name="modmm",
    )(x, modr, modr, w, col_scale)


def _modmm_t_kernel(x_ref, sh_ref, sc_ref, wt_ref, o_ref, h_ref, *, bk):
    _modulate_into(h_ref, x_ref, sh_ref, sc_ref)
    res = lax.dot_general(wt_ref[...], h_ref[...], (((1,), (1,)), ((), ())), preferred_element_type=F32)
    for j in range(o_ref.shape[0]):
        o_ref[j] = res[:, j * bk:(j + 1) * bk].astype(o_ref.dtype)


def modmm_t(x, modr, layer, nb, shift_chunk, wt, bk, tm=1024, tn=1024):
    t, d = x.shape
    n = wt.shape[0]
    tm, tn = _tile(t // nb, tm, bk), _tile(n, tn)
    tpb = (t // nb) // tm
    return pl.pallas_call(
        functools.partial(_modmm_t_kernel, bk=bk),
        grid=(t // tm, n // tn),
        in_specs=[
            pl.BlockSpec((tm, d), lambda m, j: (m, 0)),
            _mod_spec(d, layer, nb, tpb, shift_chunk),
            _mod_spec(d, layer, nb, tpb, shift_chunk + 1),
            pl.BlockSpec((tn, d), lambda m, j: (j, 0)),
        ],
        out_specs=pl.BlockSpec((tm // bk, tn, bk), lambda m, j: (m, j, 0)),
        out_shape=jax.ShapeDtypeStruct((t // bk, n, bk), BF16),
        scratch_shapes=[pltpu.VMEM((tm, d), BF16)],
        compiler_params=_params("parallel", "arbitrary"),
        name="modmm_t",
    )(x, modr, modr, wt)


def _modmm_swiglu_kernel(x_ref, sh_ref, sc_ref, wg_ref, wu_ref, o_ref, h_ref):
    _modulate_into(h_ref, x_ref, sh_ref, sc_ref)
    h = h_ref[...]
    g = jnp.dot(h, wg_ref[...], preferred_element_type=F32)
    u = jnp.dot(h, wu_ref[...], preferred_element_type=F32)
    o_ref[...] = (g * jax.nn.sigmoid(g) * u).astype(o_ref.dtype)


def modmm_swiglu(x, modr, layer, nb, shift_chunk, w_gu, tm=1024, tn=512):
    t, d = x.shape
    f = w_gu.shape[1] // 2
    tm, tn = _tile(t // nb, tm), _tile(f, tn)
    tpb = (t // nb) // tm
    nf = f // tn
    return pl.pallas_call(
        _modmm_swiglu_kernel,
        grid=(t // tm, nf),
        in_specs=[
            pl.BlockSpec((tm, d), lambda m, j: (m, 0)),
            _mod_spec(d, layer, nb, tpb, shift_chunk),
            _mod_spec(d, layer, nb, tpb, shift_chunk + 1),
            pl.BlockSpec((d, tn), lambda m, j: (0, j)),
            pl.BlockSpec((d, tn), lambda m, j: (0, j + nf)),
        ],
        out_specs=pl.BlockSpec((tm, tn), lambda m, j: (m, j)),
        out_shape=jax.ShapeDtypeStruct((t, f), BF16),
        scratch_shapes=[pltpu.VMEM((tm, d), BF16)],
        compiler_params=_params("parallel", "arbitrary"),
        name="modmm_swiglu",
    )(x, modr, modr, w_gu, w_gu)


def _res_ln(x, y, gate, ln_g, ln_b, alpha):
    v = alpha * x + (1.0 + gate) * y
    mu = jnp.mean(v, axis=-1, keepdims=True)
    dv = v - mu
    var = jnp.mean(dv * dv, axis=-1, keepdims=True)
    return dv * lax.rsqrt(var + LN_EPS) * ln_g + ln_b


LN_ROW_CHUNKS = 4


def _mm_res_ln_kernel(a_ref, w_ref, x_ref, gate_ref, lng_ref, lnb_ref, o_ref, acc_ref, *, alpha, nk):
    k = pl.program_id(1)
    if nk > 1:
        @pl.when(k == 0)
        def _():
            acc_ref[...] = jnp.dot(a_ref[...], w_ref[...], preferred_element_type=F32)

        @pl.when(jnp.logical_and(k > 0, k < nk - 1))
        def _():
            acc_ref[...] += jnp.dot(a_ref[...], w_ref[...], preferred_element_type=F32)

    @pl.when(k == nk - 1)
    def _():
        chunks = LN_ROW_CHUNKS if a_ref.shape[0] % (8 * LN_ROW_CHUNKS) == 0 else 1
        rc = a_ref.shape[0] // chunks

        def finish(c, y):
            rows = slice(c * rc, (c + 1) * rc)
            o_ref[rows, :] = _res_ln(x_ref[rows, :], y, gate_ref[0], lng_ref[...], lnb_ref[...], alpha)

        pending = None
        for c in range(chunks):
            rows = slice(c * rc, (c + 1) * rc)
            y = jnp.dot(a_ref[rows, :], w_ref[...], preferred_element_type=F32)
            if nk > 1:
                y = y + acc_ref[rows, :]
            if pending is not None:
                finish(*pending)
            pending = (c, y)
        finish(*pending)


def mm_res_ln(a, w, x, modr, layer, nb, gate_chunk, ln_g, ln_b, alpha, tm=512, tk=2048):
    t, kdim = a.shape
    d = w.shape[1]
    tm, tk = _tile(t // nb, tm), _tile(kdim, tk)
    tpb = (t // nb) // tm
    nk = kdim // tk
    return pl.pallas_call(
        functools.partial(_mm_res_ln_kernel, alpha=alpha, nk=nk),
        grid=(t // tm, nk),
        in_specs=[
            pl.BlockSpec((tm, tk), lambda m, k: (m, k)),
            pl.BlockSpec((tk, d), lambda m, k: (k, 0)),
            pl.BlockSpec((tm, d), lambda m, k: (m, 0)),
            _mod_spec(d, layer, nb, tpb, gate_chunk),
            pl.BlockSpec((1, d), lambda m, k: (0, 0)),
            pl.BlockSpec((1, d), lambda m, k: (0, 0)),
        ],
        out_specs=pl.BlockSpec((tm, d), lambda m, k: (m, 0)),
        out_shape=jax.ShapeDtypeStruct((t, d), F32),
        scratch_shapes=[pltpu.VMEM((tm, d), F32)],
        compiler_params=_params("parallel", "arbitrary"),
        name="mm_res_ln",
    )(a, w, x, modr, ln_g.reshape(1, d), ln_b.reshape(1, d))


def _head_cols(h):
    return slice(h * HEAD_DIM, (h + 1) * HEAD_DIM)


def _sb_attn_kernel(q_ref, k_ref, vt_ref, o_ref, *, bk, nq, heads):
    tq = 2 * bk
    key = lax.broadcasted_iota(jnp.int32, (bk, tq), 0)
    qry = lax.broadcasted_iota(jnp.int32, (bk, tq), 1)
    r2 = lax.broadcasted_iota(jnp.int32, (bk, bk), 0)
    c2 = lax.broadcasted_iota(jnp.int32, (bk, bk), 1)
    later = (c2 > r2).astype(BF16)
    later2 = jnp.concatenate([later, later], axis=1)

    def blocks(qs, kbs, offsets, carries):
        hs = range(heads)
        units = [(h, i) for i in range(len(kbs)) for h in hs]
        masks = [None if off is None else key + off < qry for off in offsets]
        zs = [lax.dot_general(k_ref[pl.ds(pl.multiple_of(kbs[i] * bk, bk), bk), _head_cols(h)], qs[h],
                              (((1,), (1,)), ((), ())), preferred_element_type=F32) for h, i in units]
        bases, splits, sums = [], [], []
        for (h, i), z in zip(units, zs):
            neg_abs = pltpu.bitcast(pltpu.bitcast(z, jnp.uint32) | jnp.uint32(0x80000000), F32)
            sp = jnp.maximum(z, 0.0) + jnp.log(1.0 + jnp.exp2(neg_abs)) * LOG2_E
            bases.append(z - sp)
            if masks[i] is not None:
                sp = jnp.where(masks[i], sp, 0.0)
            hi = sp.astype(BF16)
            lo = (sp - hi.astype(F32)).astype(BF16)
            splits.append(jnp.concatenate([hi, lo], axis=0))
            sums.append(jnp.sum(sp, axis=0, keepdims=True))
        tails = [jnp.dot(later2, split, preferred_element_type=F32) for split in splits]
        runs = [carries[h][0] for h in hs]
        probs = []
        for u, (h, i) in enumerate(units):
            a = jnp.exp2(bases[u] - tails[u] - runs[h])
            if masks[i] is not None:
                a = jnp.where(masks[i], a, 0.0)
            probs.append(a.astype(BF16))
            runs[h] = runs[h] + sums[u]
        accs = [carries[h][1] for h in hs]
        for u, (h, i) in enumerate(units):
            accs[h] = accs[h] + jnp.dot(vt_ref[kbs[i], _head_cols(h), :], probs[u], preferred_element_type=F32)
        return tuple(zip(runs, accs))

    def qbody(qi, _):
        q0 = pl.multiple_of(qi * tq, tq)
        qs = [q_ref[pl.ds(q0, tq), _head_cols(h)] for h in range(heads)]
        carries = tuple((jnp.zeros((1, tq), F32), jnp.zeros((HEAD_DIM, tq), F32)) for _ in range(heads))
        carries = blocks(qs, [2 * qi + 1, 2 * qi], [bk, 0], carries)

        def kbody(i, carries):
            pair = qi - 1 - i
            return blocks(qs, [2 * pair + 1, 2 * pair], [None, None], carries)

        carries = lax.fori_loop(0, qi, kbody, carries)
        for h in range(heads):
            o_ref[pl.ds(q0, tq), _head_cols(h)] = carries[h][1].T.astype(o_ref.dtype)
        return 0

    lax.fori_loop(0, nq, qbody, 0)


def sb_attention(qk, vt, nb, heads=4):
    t, n2 = qk.shape
    s = t // nb
    nh = n2 // (2 * HEAD_DIM)
    bk = vt.shape[2]
    assert s % (2 * bk) == 0
    heads = min(heads, nh)
    ng = nh // heads
    cols = heads * HEAD_DIM
    return pl.pallas_call(
        functools.partial(_sb_attn_kernel, bk=bk, nq=s // (2 * bk), heads=heads),
        grid=(nb, ng),
        in_specs=[
            pl.BlockSpec((s, cols), lambda b, g: (b, g)),
            pl.BlockSpec((s, cols), lambda b, g: (b, ng + g)),
            pl.BlockSpec((s // bk, cols, bk), lambda b, g: (b, g, 0)),
        ],
        out_specs=pl.BlockSpec((s, cols), lambda b, g: (b, g)),
        out_shape=jax.ShapeDtypeStruct((t, nh * HEAD_DIM), BF16),
        compiler_params=_params("parallel", "parallel"),
        name="sb_attention",
    )(qk, qk, vt)


def _mla_in_kernel(x_ref, sh_ref, sc_ref, w_ref, qn_ref, kvn_ref, cos_ref, sin_ref,
                   cq_ref, ckv_ref, kr_ref, *, q_rank, kv_rank):
    h = (x_ref[...] * (1.0 + sc_ref[0]) + sh_ref[0]).astype(BF16)
    lat = jnp.dot(h, w_ref[...], preferred_element_type=F32)

    def rms(v, g):
        return v * lax.rsqrt(jnp.mean(v * v, axis=-1, keepdims=True) + RMS_EPS) * g

    cq_ref[...] = rms(lat[:, :q_rank], qn_ref[...]).astype(cq_ref.dtype)
    ckv_ref[...] = rms(lat[:, q_rank:q_rank + kv_rank], kvn_ref[...]).astype(ckv_ref.dtype)
    kr_ref[...] = _rope(lat[:, q_rank + kv_rank:], cos_ref[...], sin_ref[...]).astype(kr_ref.dtype)


def mla_in(x, modr, layer, nb, w_in_pad, q_norm, kv_norm, cos, sin, tm=512):
    t, d = x.shape
    q_rank, kv_rank = q_norm.shape[0], kv_norm.shape[0]
    n = w_in_pad.shape[1]
    assert n == q_rank + kv_rank + LANES
    tm = _tile(t // nb, tm)
    tpb = (t // nb) // tm
    return pl.pallas_call(
        functools.partial(_mla_in_kernel, q_rank=q_rank, kv_rank=kv_rank),
        grid=(t // tm,),
        in_specs=[
            pl.BlockSpec((tm, d), lambda m: (m, 0)),
            _mod_spec(d, layer, nb, tpb, 0),
            _mod_spec(d, layer, nb, tpb, 1),
            pl.BlockSpec((d, n), lambda m: (0, 0)),
            pl.BlockSpec((1, q_rank), lambda m: (0, 0)),
            pl.BlockSpec((1, kv_rank), lambda m: (0, 0)),
            pl.BlockSpec((tm, LANES), lambda m: (m, 0)),
            pl.BlockSpec((tm, LANES), lambda m: (m, 0)),
        ],
        out_specs=[
            pl.BlockSpec((tm, q_rank), lambda m: (m, 0)),
            pl.BlockSpec((tm, kv_rank), lambda m: (m, 0)),
            pl.BlockSpec((tm, LANES), lambda m: (m, 0)),
        ],
        out_shape=[
            jax.ShapeDtypeStruct((t, q_rank), BF16),
            jax.ShapeDtypeStruct((t, kv_rank), BF16),
            jax.ShapeDtypeStruct((t, LANES), BF16),
        ],
        compiler_params=_params("parallel"),
        name="mla_in",
    )(x, modr, modr, w_in_pad, q_norm.reshape(1, q_rank), kv_norm.reshape(1, kv_rank), cos, sin)


def _mm_kernel(a_ref, w_ref, o_ref):
    o_ref[...] = jnp.dot(a_ref[...], w_ref[...], preferred_element_type=F32).astype(o_ref.dtype)


def mm(a, w, tm=1024, tn=1024):
    t, k = a.shape
    n = w.shape[1]
    tm, tn = _tile(t, tm), _tile(n, tn)
    return pl.pallas_call(
        _mm_kernel,
        grid=(t // tm, n // tn),
        in_specs=[pl.BlockSpec((tm, k), lambda m, j: (m, 0)), pl.BlockSpec((k, tn), lambda m, j: (0, j))],
        out_specs=pl.BlockSpec((tm, tn), lambda m, j: (m, j)),
        out_shape=jax.ShapeDtypeStruct((t, n), BF16),
        compiler_params=_params("parallel", "parallel"),
        name="mm",
    )(a, w)


def _mm_q_rope_kernel(a_ref, w_ref, cos_ref, sin_ref, o_ref, *, scale, heads):
    acc = jnp.dot(a_ref[...], w_ref[...], preferred_element_type=F32)
    cos, sin = cos_ref[...], sin_ref[...]
    for h in range(heads):
        c0 = 2 * LANES * h
        o_ref[:, c0:c0 + LANES] = (acc[:, c0:c0 + LANES] * scale).astype(o_ref.dtype)
        roped = _rope(acc[:, c0 + LANES:c0 + 2 * LANES], cos, sin)
        o_ref[:, c0 + LANES:c0 + 2 * LANES] = (roped * scale).astype(o_ref.dtype)


def mm_q_rope(a, w, cos, sin, scale, tm=1024, heads_per_tile=4):
    t, k = a.shape
    n = w.shape[1]
    tm = _tile(t, tm)
    tn = _tile(n, 2 * LANES * heads_per_tile)
    return pl.pallas_call(
        functools.partial(_mm_q_rope_kernel, scale=scale, heads=tn // (2 * LANES)),
        grid=(t // tm, n // tn),
        in_specs=[
            pl.BlockSpec((tm, k), lambda m, j: (m, 0)),
            pl.BlockSpec((k, tn), lambda m, j: (0, j)),
            pl.BlockSpec((tm, LANES), lambda m, j: (m, 0)),
            pl.BlockSpec((tm, LANES), lambda m, j: (m, 0)),
        ],
        out_specs=pl.BlockSpec((tm, tn), lambda m, j: (m, j)),
        out_shape=jax.ShapeDtypeStruct((t, n), BF16),
        compiler_params=_params("parallel", "parallel"),
        name="mm_q_rope",
    )(a, w, cos, sin)


def _mm_t_kernel(a_ref, wt_ref, o_ref, *, bk):
    res = lax.dot_general(wt_ref[...], a_ref[...], (((1,), (1,)), ((), ())), preferred_element_type=F32)
    for j in range(o_ref.shape[0]):
        o_ref[j] = res[:, j * bk:(j + 1) * bk].astype(o_ref.dtype)


def mm_t(a, wt, bk, tm=1024, tn=1024):
    t, k = a.shape
    n = wt.shape[0]
    tm, tn = _tile(t, tm, bk), _tile(n, tn)
    return pl.pallas_call(
        functools.partial(_mm_t_kernel, bk=bk),
        grid=(t // tm, n // tn),
        in_specs=[pl.BlockSpec((tm, k), lambda m, j: (m, 0)), pl.BlockSpec((tn, k), lambda m, j: (j, 0))],
        out_specs=pl.BlockSpec((tm // bk, tn, bk), lambda m, j: (m, j, 0)),
        out_shape=jax.ShapeDtypeStruct((t // bk, n, bk), BF16),
        compiler_params=_params("parallel", "parallel"),
        name="mm_t",
    )(a, wt)


def _mla_attn_kernel(q_ref, kn_ref, kr_ref, vt_ref, o_ref, kcat_ref, *, tq, nq, heads):
    for h in range(heads):
        c0 = 2 * HEAD_DIM * h
        kcat_ref[:, c0:c0 + HEAD_DIM] = kn_ref[:, _head_cols(h)]
        kcat_ref[:, c0 + HEAD_DIM:c0 + 2 * HEAD_DIM] = kr_ref[...]
    key = lax.broadcasted_iota(jnp.int32, (tq, tq), 0)
    qry = lax.broadcasted_iota(jnp.int32, (tq, tq), 1)
    causal = key <= qry

    def blocks(qs, kb, carries, masked):
        hs = range(heads)
        k0 = pl.multiple_of(kb * tq, tq)
        ss = [lax.dot_general(kcat_ref[pl.ds(k0, tq), 2 * HEAD_DIM * h:2 * HEAD_DIM * (h + 1)], qs[h],
                              (((1,), (1,)), ((), ())), preferred_element_type=F32) for h in hs]
        if masked:
            ss = [jnp.where(causal, s, -jnp.inf) for s in ss]
        m_new, ps, corrs, ls = [], [], [], []
        for h in hs:
            m_old, l_old, _ = carries[h]
            m = jnp.maximum(m_old, jnp.max(ss[h], axis=0, keepdims=True))
            p = jnp.exp2(ss[h] - m)
            corr = jnp.exp2(m_old - m)
            m_new.append(m)
            corrs.append(corr)
            ls.append(l_old * corr + jnp.sum(p, axis=0, keepdims=True))
            ps.append(p.astype(BF16))
        accs = [carries[h][2] * corrs[h] + jnp.dot(vt_ref[kb, _head_cols(h), :], ps[h], preferred_element_type=F32)
                for h in hs]
        return tuple(zip(m_new, ls, accs))

    def qbody(qi, _):
        q0 = pl.multiple_of(qi * tq, tq)
        qs = [q_ref[pl.ds(q0, tq), 2 * HEAD_DIM * h:2 * HEAD_DIM * (h + 1)] for h in range(heads)]
        carries = tuple((jnp.full((1, tq), -jnp.inf, F32), jnp.zeros((1, tq), F32), jnp.zeros((HEAD_DIM, tq), F32))
                        for _ in range(heads))
        carries = blocks(qs, qi, carries, True)

        def kbody(i, carries):
            return blocks(qs, i, carries, False)

        carries = lax.fori_loop(0, qi, kbody, carries)
        for h in range(heads):
            _, l, acc = carries[h]
            o_ref[pl.ds(q0, tq), _head_cols(h)] = (acc / l).T.astype(o_ref.dtype)
        return 0

    lax.fori_loop(0, nq, qbody, 0)


def mla_attention(q, kn, kr, vt, nb, heads=4):
    t = q.shape[0]
    s = t // nb
    nh = q.shape[1] // (2 * HEAD_DIM)
    tq = vt.shape[2]
    heads = min(heads, nh)
    return pl.pallas_call(
        functools.partial(_mla_attn_kernel, tq=tq, nq=s // tq, heads=heads),
        grid=(nb, nh // heads),
        in_specs=[
            pl.BlockSpec((s, 2 * HEAD_DIM * heads), lambda b, g: (b, g)),
            pl.BlockSpec((s, HEAD_DIM * heads), lambda b, g: (b, g)),
            pl.BlockSpec((s, LANES), lambda b, g: (b, 0)),
            pl.BlockSpec((s // tq, HEAD_DIM * heads, tq), lambda b, g: (b, g, 0)),
        ],
        out_specs=pl.BlockSpec((s, HEAD_DIM * heads), lambda b, g: (b, g)),
        out_shape=jax.ShapeDtypeStruct((t, nh * HEAD_DIM), BF16),
        scratch_shapes=[pltpu.VMEM((s, 2 * HEAD_DIM * heads), BF16)],
        compiler_params=_params("parallel", "parallel"),
        name="mla_attention",
    )(q, kn, kr, vt)


def _router_kernel(x_ref, sh_ref, sc_ref, wr_ref, h_ref, idx_ref, gate_ref, cnt_ref, tri_ref, base_ref,
                   *, tm, n_exp):
    @pl.when(pl.program_id(0) == 0)
    def _():
        r = lax.broadcasted_iota(jnp.int32, (tm, tm), 0)
        c = lax.broadcasted_iota(jnp.int32, (tm, tm), 1)
        tri_ref[...] = (c <= r).astype(BF16)
        base_ref[...] = jnp.zeros_like(base_ref)

    h = x_ref[...] * (1.0 + sc_ref[0]) + sh_ref[0]
    ns = h.shape[1] // LANES
    for c in range(ns):
        h_ref[pl.ds(c, tm, stride=ns), :] = h[:, c * LANES:(c + 1) * LANES]
    h_hi = h.astype(BF16)
    h_lo = (h - h_hi.astype(F32)).astype(BF16)
    w = wr_ref[...]
    w_hi = w.astype(BF16)
    w_lo = (w - w_hi.astype(F32)).astype(BF16)
    logits = (jnp.dot(h_hi, w_hi, preferred_element_type=F32)
              + jnp.dot(h_lo, w_hi, preferred_element_type=F32)
              + jnp.dot(h_hi, w_lo, preferred_element_type=F32))
    lane = lax.broadcasted_iota(jnp.int32, logits.shape, 1).astype(F32)
    lg = jnp.where(lane < n_exp, logits, -jnp.inf)
    m1 = jnp.max(lg, axis=-1, keepdims=True)
    i1 = jnp.min(jnp.where(lg == m1, lane, float(LANES)), axis=-1, keepdims=True)
    lg2 = jnp.where(lane == i1, -jnp.inf, lg)
    m2 = jnp.max(lg2, axis=-1, keepdims=True)
    i2 = jnp.min(jnp.where(lg2 == m2, lane, float(LANES)), axis=-1, keepdims=True)
    e = jnp.exp(m2 - m1)
    g1 = 1.0 / (1.0 + e)
    g2 = e * g1
    oh1 = lane == i1
    oh2 = lane == i2
    onehot = jnp.where(oh1 | oh2, 1.0, 0.0).astype(BF16)
    tot = base_ref[...] + jnp.dot(tri_ref[...], onehot, preferred_element_type=F32)
    r1 = jnp.sum(jnp.where(oh1, tot, 0.0), axis=-1, keepdims=True) - 1.0
    r2 = jnp.sum(jnp.where(oh2, tot, 0.0), axis=-1, keepdims=True) - 1.0
    last = tot[tm - 1:tm, :]
    base_ref[...] = last
    cnt_ref[...] = jnp.broadcast_to(last, cnt_ref.shape)
    info = jnp.where(lane == 0, i1, jnp.where(lane == 1, i2, jnp.where(lane == 2, r1, jnp.where(lane == 3, r2, 0.0))))
    idx_ref[...] = info.astype(jnp.int32)
    gate_ref[...] = jnp.where(lane == 0, g1, jnp.where(lane == 1, g2, 0.0))


def router(x, modr, layer, nb, w_router, tm=512):
    t, d = x.shape
    n_exp = w_router.shape[1]
    tm = _tile(t // nb, tm)
    tpb = (t // nb) // tm
    ns = d // LANES
    wr = jnp.pad(w_router, ((0, 0), (0, LANES - n_exp)))
    return pl.pallas_call(
        functools.partial(_router_kernel, tm=tm, n_exp=n_exp),
        grid=(t // tm,),
        in_specs=[
            pl.BlockSpec((tm, d), lambda m: (m, 0)),
            _mod_spec(d, layer, nb, tpb, 3),
            _mod_spec(d, layer, nb, tpb, 4),
            pl.BlockSpec((d, LANES), lambda m: (0, 0)),
        ],
        out_specs=[
            pl.BlockSpec((tm * ns, LANES), lambda m: (m, 0)),
            pl.BlockSpec((tm, LANES), lambda m: (m, 0)),
            pl.BlockSpec((tm, LANES), lambda m: (m, 0)),
            pl.BlockSpec((8, LANES), lambda m: (0, 0)),
        ],
        out_shape=[
            jax.ShapeDtypeStruct((t * ns, LANES), F32),
            jax.ShapeDtypeStruct((t, LANES), jnp.int32),
            jax.ShapeDtypeStruct((t, LANES), F32),
            jax.ShapeDtypeStruct((8, LANES), F32),
        ],
        scratch_shapes=[pltpu.VMEM((tm, tm), BF16), pltpu.VMEM((1, LANES), F32)],
        compiler_params=_params("arbitrary"),
        name="router",
    )(x, modr, modr, wr)


ROW_DMA_UNROLL = 8


def _slab(ref, row, ns):
    return ref.at[pl.ds(pl.multiple_of(row * ns, ns), ns), :]


def _gather_rows_kernel(idx_ref, idx_next_ref, src_ref, o_ref, buf_ref, sem, *, tg, ns, steps):
    i = pl.program_id(0)
    slot = i % 2

    def issue(ids_ref, s):
        def start(r, _):
            pltpu.make_async_copy(_slab(src_ref, ids_ref[0, 0, r], ns), _slab(buf_ref, s * tg + r, ns),
                                  sem.at[s]).start()
            return 0
        lax.fori_loop(0, tg, start, 0, unroll=ROW_DMA_UNROLL)

    @pl.when(i == 0)
    def _():
        issue(idx_ref, 0)

    @pl.when(i + 1 < steps)
    def _():
        issue(idx_next_ref, 1 - slot)

    base = pl.multiple_of(slot * (tg * ns), tg * ns)
    pltpu.make_async_copy(src_ref.at[pl.ds(0, tg * ns), :], buf_ref.at[pl.ds(base, tg * ns), :], sem.at[slot]).wait()
    for c in range(ns):
        o_ref[:, c * LANES:(c + 1) * LANES] = buf_ref[pl.ds(base + c, tg, stride=ns), :].astype(o_ref.dtype)


def gather_rows(src, idx, ns, tg=512):
    n = idx.shape[0]
    tg = _tile(n, tg)
    steps = n // tg
    ids = idx.reshape(steps, 1, tg)
    return pl.pallas_call(
        functools.partial(_gather_rows_kernel, tg=tg, ns=ns, steps=steps),
        grid=(steps,),
        in_specs=[
            pl.BlockSpec((1, 1, tg), lambda m: (m, 0, 0), memory_space=pltpu.SMEM),
            pl.BlockSpec((1, 1, tg), lambda m: (jnp.minimum(m + 1, steps - 1), 0, 0), memory_space=pltpu.SMEM),
            pl.BlockSpec(memory_space=pl.ANY),
        ],
        out_specs=pl.BlockSpec((tg, ns * LANES), lambda m: (m, 0)),
        out_shape=jax.ShapeDtypeStruct((n, ns * LANES), BF16),
        scratch_shapes=[pltpu.VMEM((2 * tg * ns, LANES), src.dtype), pltpu.SemaphoreType.DMA((2,))],
        compiler_params=_params("arbitrary"),
        name="gather_rows",
    )(ids, ids, src)


def _gmm_swiglu_kernel(te_ref, nu_ref, a_ref, wg_ref, wu_ref, o_ref, wgb_ref, wub_ref):
    m = pl.program_id(1)
    used = m < nu_ref[0]
    fresh = jnp.logical_or(m == 0, te_ref[m] != te_ref[jnp.maximum(m - 1, 0)])

    @pl.when(fresh)
    def _():
        wgb_ref[...] = wg_ref[...].astype(BF16)
        wub_ref[...] = wu_ref[...].astype(BF16)

    @pl.when(used)
    def _():
        a = a_ref[...]
        g = jnp.dot(a, wgb_ref[...], preferred_element_type=F32)
        u = jnp.dot(a, wub_ref[...], preferred_element_type=F32)
        o_ref[...] = (g * jax.nn.sigmoid(g) * u).astype(o_ref.dtype)

    @pl.when(jnp.logical_not(used))
    def _():
        o_ref[...] = jnp.zeros_like(o_ref)


def gmm_swiglu(a, w_gu, layer, tile_expert, n_used, tm, tn=1024):
    ms, d = a.shape
    f = w_gu.shape[3] // 2
    tn = _tile(f, tn)
    nf = f // tn
    return pl.pallas_call(
        _gmm_swiglu_kernel,
        grid_spec=pltpu.PrefetchScalarGridSpec(
            num_scalar_prefetch=2,
            grid=(nf, ms // tm),
            in_specs=[
                pl.BlockSpec((tm, d), lambda j, m, te, nu: (m, 0)),
                pl.BlockSpec((None, None, d, tn), lambda j, m, te, nu: (layer, te[m], 0, j)),
                pl.BlockSpec((None, None, d, tn), lambda j, m, te, nu: (layer, te[m], 0, j + nf)),
            ],
            out_specs=pl.BlockSpec((tm, tn), lambda j, m, te, nu: (m, j)),
            scratch_shapes=[pltpu.VMEM((d, tn), BF16), pltpu.VMEM((d, tn), BF16)],
        ),
        out_shape=jax.ShapeDtypeStruct((ms, f), BF16),
        compiler_params=_params("parallel", "arbitrary"),
        name="gmm_swiglu",
    )(tile_expert, n_used, a, w_gu, w_gu)


def _gmm_kernel(te_ref, nu_ref, a_ref, w_ref, o_ref):
    used = pl.program_id(1) < nu_ref[0]

    @pl.when(used)
    def _():
        acc = jnp.dot(a_ref[...], w_ref[...], preferred_element_type=F32)
        for c in range(o_ref.shape[1]):
            o_ref[:, c, :] = acc[:, c * LANES:(c + 1) * LANES]

    @pl.when(jnp.logical_not(used))
    def _():
        o_ref[...] = jnp.zeros_like(o_ref)


def gmm(a, w, layer, tile_expert, n_used, tm, tn=1024):
    ms, k = a.shape
    n = w.shape[3]
    tn = _tile(n, tn, 8 * LANES)
    return pl.pallas_call(
        _gmm_kernel,
        grid_spec=pltpu.PrefetchScalarGridSpec(
            num_scalar_prefetch=2,
            grid=(n // tn, ms // tm),
            in_specs=[
                pl.BlockSpec((tm, k), lambda j, m, te, nu: (m, 0)),
                pl.BlockSpec((None, None, k, tn), lambda j, m, te, nu: (layer, te[m], 0, j)),
            ],
            out_specs=pl.BlockSpec((tm, tn // LANES, LANES), lambda j, m, te, nu: (m, j, 0)),
        ),
        out_shape=jax.ShapeDtypeStruct((ms, n // LANES, LANES), F32),
        compiler_params=_params("parallel", "arbitrary"),
        name="gmm",
    )(tile_expert, n_used, a, w)


def _combine_ln_kernel(p0_ref, p1_ref, p0_next_ref, p1_next_ref, y_ref, gate_ref, x_ref, gmod_ref, lng_ref, lnb_ref,
                       o_ref, ybuf_ref, ys_ref, sem, *, tm, ns, steps, alpha):
    i = pl.program_id(0)
    slot = i % 2

    def issue(pa_ref, pb_ref, s):
        def start(r, _):
            row = (s * TOP_K) * tm + r
            pltpu.make_async_copy(_slab(y_ref, pa_ref[0, 0, r], ns), _slab(ybuf_ref, row, ns), sem.at[s]).start()
            pltpu.make_async_copy(_slab(y_ref, pb_ref[0, 0, r], ns), _slab(ybuf_ref, row + tm, ns), sem.at[s]).start()
            return 0
        lax.fori_loop(0, tm, start, 0, unroll=ROW_DMA_UNROLL)

    @pl.when(i == 0)
    def _():
        issue(p0_ref, p1_ref, 0)

    @pl.when(i + 1 < steps)
    def _():
        issue(p0_next_ref, p1_next_ref, 1 - slot)

    rows = TOP_K * tm * ns
    base = pl.multiple_of(slot * rows, rows)
    pltpu.make_async_copy(y_ref.at[pl.ds(0, rows), :], ybuf_ref.at[pl.ds(base, rows), :], sem.at[slot]).wait()
    gates = gate_ref[...]
    g0, g1 = gates[:, 0:1], gates[:, 1:2]
    for c in range(ns):
        y0 = ybuf_ref[pl.ds(base + c, tm, stride=ns), :]
        y1 = ybuf_ref[pl.ds(base + tm * ns + c, tm, stride=ns), :]
        ys_ref[:, c * LANES:(c + 1) * LANES] = g0 * y0 + g1 * y1
    o_ref[...] = _res_ln(x_ref[...], ys_ref[...], gmod_ref[0], lng_ref[...], lnb_ref[...], alpha)


def combine_ln(y_slabs, pos0, pos1, gates, x, modr, layer, nb, ln_g, ln_b, alpha, tm=256):
    t, d = x.shape
    tm = _tile(t // nb, tm)
    tpb = (t // nb) // tm
    nt = t // tm
    ns = d // LANES
    ids_spec = pl.BlockSpec((1, 1, tm), lambda m: (m, 0, 0), memory_space=pltpu.SMEM)
    ids_next_spec = pl.BlockSpec((1, 1, tm), lambda m: (jnp.minimum(m + 1, nt - 1), 0, 0), memory_space=pltpu.SMEM)
    p0, p1 = pos0.reshape(nt, 1, tm), pos1.reshape(nt, 1, tm)
    return pl.pallas_call(
        functools.partial(_combine_ln_kernel, tm=tm, ns=ns, steps=nt, alpha=alpha),
        grid=(nt,),
        in_specs=[
            ids_spec,
            ids_spec,
            ids_next_spec,
            ids_next_spec,
            pl.BlockSpec(memory_space=pl.ANY),
            pl.BlockSpec((tm, LANES), lambda m: (m, 0)),
            pl.BlockSpec((tm, d), lambda m: (m, 0)),
            _mod_spec(d, layer, nb, tpb, 5),
            pl.BlockSpec((1, d), lambda m: (0, 0)),
            pl.BlockSpec((1, d), lambda m: (0, 0)),
        ],
        out_specs=pl.BlockSpec((tm, d), lambda m: (m, 0)),
        out_shape=jax.ShapeDtypeStruct((t, d), F32),
        scratch_shapes=[pltpu.VMEM((2 * TOP_K * tm * ns, LANES), F32), pltpu.VMEM((tm, d), F32),
                        pltpu.SemaphoreType.DMA((2,))],
        compiler_params=_params("arbitrary"),
        name="combine_ln",
    )(p0, p1, p0, p1, y_slabs.reshape(-1, LANES), gates, x, modr, ln_g.reshape(1, d), ln_b.reshape(1, d))


def moe_layer(x, modr, layer, nb, w_router, w_gu, w_down, moe_index, ln_g, ln_b, alpha, tm_g=512):
    t, d = x.shape
    n_exp = w_router.shape[1]
    h, info, gates, cnt = router(x, modr, layer, nb, w_router)
    counts = cnt[0, :n_exp].astype(jnp.int32)
    sizes = (counts + tm_g - 1) // tm_g * tm_g
    ends = jnp.cumsum(sizes)
    starts = ends - sizes
    pos0 = starts[info[:, 0]] + info[:, 2]
    pos1 = starts[info[:, 1]] + info[:, 3]
    ms = t * TOP_K + n_exp * tm_g
    tok = jnp.arange(t, dtype=jnp.int32)
    slot_token = jnp.zeros((ms,), jnp.int32).at[jnp.concatenate([pos0, pos1])].set(jnp.concatenate([tok, tok]))
    tile_start = jnp.arange(ms // tm_g, dtype=jnp.int32) * tm_g
    tile_expert = jnp.minimum(jnp.sum(tile_start[:, None] >= ends[None, :], axis=1), n_exp - 1).astype(jnp.int32)
    n_used = (ends[-1:] // tm_g).astype(jnp.int32)

    h_sorted = gather_rows(h, slot_token, d // LANES)
    act = gmm_swiglu(h_sorted, w_gu, moe_index, tile_expert, n_used, tm_g)
    y_slabs = gmm(act, w_down, moe_index, tile_expert, n_used, tm_g)
    return combine_ln(y_slabs, pos0, pos1, gates, x, modr, layer, nb, ln_g, ln_b, alpha)


def kernel(x, c, positions, mod_w, mod_b, ln_g, ln_b, sb_w_qkv, sb_w_o, mla_w_in, mla_q_norm, mla_kv_norm,
           mla_w_uq, mla_w_ukv, mla_w_o, ffn_w_gu, ffn_w_down, moe_router, moe_w_gu, moe_w_down):
    nb, s, d = x.shape
    depth = mod_w.shape[0]
    t = nb * s
    alpha = float((2 * depth) ** 0.25)
    xf = x.reshape(t, d)

    modr = mod_table(c, mod_w, mod_b)
    cos, sin = rope_tables(positions)

    sb_heads = sb_w_o.shape[1] // HEAD_DIM
    sb_scale = HEAD_DIM ** -0.5 * LOG2_E
    sb_col_scale = jnp.concatenate([jnp.full((sb_heads * HEAD_DIM,), sb_scale, F32),
                                    jnp.ones((2 * sb_heads * HEAD_DIM,), F32)]).reshape(1, -1)

    mla_heads = mla_w_o.shape[1] // HEAD_DIM
    q_rank = mla_q_norm.shape[1]
    mla_scale = float((HEAD_DIM + ROPE_DIM) ** -0.5) * LOG2_E
    moe_w_down_bf16 = moe_w_down.astype(BF16)

    for i in range(depth):
        j = i // 2
        if i % 2 == 0:
            n_qk = 2 * sb_heads * HEAD_DIM
            qk = modmm(xf, modr, i, nb, 0, sb_w_qkv[j, :, :n_qk].astype(BF16), sb_col_scale[:, :n_qk])
            vt = modmm_t(xf, modr, i, nb, 0, sb_w_qkv[j, :, n_qk:].T.astype(BF16), _tile(s, SB_KEY_BLOCK))
            o = sb_attention(qk, vt, nb)
            w_o = sb_w_o[j]
        else:
            w_in = jnp.pad(mla_w_in[j], ((0, 0), (0, LANES - ROPE_DIM))).astype(BF16)
            cq, ckv, kr = mla_in(xf, modr, i, nb, w_in, mla_q_norm[j], mla_kv_norm[j], cos, sin)
            w_uq = mla_w_uq[j].reshape(q_rank, mla_heads, HEAD_DIM + ROPE_DIM)
            w_uq = jnp.pad(w_uq, ((0, 0), (0, 0), (0, LANES - ROPE_DIM))).reshape(q_rank, mla_heads * 2 * LANES)
            q = mm_q_rope(cq, w_uq.astype(BF16), cos, sin, mla_scale)
            w_ukv = mla_w_ukv[j].reshape(-1, mla_heads, 2 * HEAD_DIM)
            w_uk = w_ukv[:, :, :HEAD_DIM].reshape(-1, mla_heads * HEAD_DIM).astype(BF16)
            w_uv_t = w_ukv[:, :, HEAD_DIM:].reshape(-1, mla_heads * HEAD_DIM).T.astype(BF16)
            kn = mm(ckv, w_uk)
            vt = mm_t(ckv, w_uv_t, _tile(s, ATTN_BLOCK))
            o = mla_attention(q, kn, kr, vt, nb)
            w_o = mla_w_o[j]
        xf = mm_res_ln(o, w_o.astype(BF16), xf, modr, i, nb, 2, ln_g[i, 0], ln_b[i, 0], alpha)
        if i % 2 == 0:
            act = modmm_swiglu(xf, modr, i, nb, 3, ffn_w_gu[j].astype(BF16))
            xf = mm_res_ln(act, ffn_w_down[j].astype(BF16), xf, modr, i, nb, 5, ln_g[i, 1], ln_b[i, 1], alpha)
        else:
            xf = moe_layer(xf, modr, i, nb, moe_router[j], moe_w_gu, moe_w_down_bf16, j,
                           ln_g[i, 1], ln_b[i, 1], alpha)
    return xf.reshape(nb, s, d)
```

```python
import functools

import jax
import jax.numpy as jnp
from jax import lax
from jax.experimental import pallas as pl
from jax.experimental.pallas import tpu as pltpu

F32 = jnp.float32
BF16 = jnp.bfloat16

HEAD_DIM = 128
ROPE_DIM = 64
ROPE_HALF = ROPE_DIM // 2
ROPE_THETA = 10000.0
N_MOD = 6
TOP_K = 2
LN_EPS = 1e-5
RMS_EPS = 1e-6
LOG2_E = 1.4426950408889634
ATTN_BLOCK = 512
SB_KEY_BLOCK = 256
LANES = 128
VMEM_LIMIT = 56 * 1024 * 1024


def _tile(n, want, align=LANES):
    if n <= want:
        return n
    t = want - want % align
    while n % t:
        t -= align
    return t


def _params(*sem):
    return pltpu.CompilerParams(dimension_semantics=sem, vmem_limit_bytes=VMEM_LIMIT)


def _mod_kernel(c_ref, w_ref, b_ref, o_ref):
    c = c_ref[...]
    cond = (c * jax.nn.sigmoid(c)).astype(BF16)
    o_ref[...] = jnp.dot(cond, w_ref[...].astype(BF16), preferred_element_type=F32) + b_ref[...]


def mod_table(c, mod_w, mod_b):
    depth, d, n = mod_w.shape
    b = c.shape[0]
    tn = _tile(n, 1024)
    out = pl.pallas_call(
        _mod_kernel,
        grid=(depth, n // tn),
        in_specs=[
            pl.BlockSpec((b, d), lambda i, j: (0, 0)),
            pl.BlockSpec((None, d, tn), lambda i, j: (i, 0, j)),
            pl.BlockSpec((None, 1, tn), lambda i, j: (i, 0, j)),
        ],
        out_specs=pl.BlockSpec((None, b, tn), lambda i, j: (i, 0, j)),
        out_shape=jax.ShapeDtypeStruct((depth, b, n), F32),
        compiler_params=_params("parallel", "parallel"),
        name="mod_table",
    )(c, mod_w, mod_b.reshape(depth, 1, n))
    return out.reshape(depth * b * N_MOD, 1, d)


def _mod_spec(d, layer, nb, tiles_per_batch, chunk):
    def index(m, *_):
        return ((layer * nb + m // tiles_per_batch) * N_MOD + chunk, 0, 0)
    return pl.BlockSpec((1, 1, d), index)


def _rope_table_kernel(pos_ref, invf_ref, cos_ref, sin_ref):
    ang = pos_ref[...].astype(F32) * invf_ref[...]
    lane = lax.broadcasted_iota(jnp.int32, ang.shape, 1)
    live = lane < ROPE_DIM
    cos_ref[...] = jnp.where(live, jnp.cos(ang), 0.0)
    sin_ref[...] = jnp.where(live, jnp.sin(ang), 0.0)


def rope_tables(positions):
    t = positions.size
    tm = _tile(t, 1024)
    inv_freq = ROPE_THETA ** (-jnp.arange(ROPE_HALF, dtype=F32) / ROPE_HALF)
    invf = jnp.concatenate([inv_freq, inv_freq, jnp.zeros((LANES - ROPE_DIM,), F32)]).reshape(1, LANES)
    return pl.pallas_call(
        _rope_table_kernel,
        grid=(t // tm,),
        in_specs=[pl.BlockSpec((tm, 1), lambda m: (m, 0)), pl.BlockSpec((1, LANES), lambda m: (0, 0))],
        out_specs=[pl.BlockSpec((tm, LANES), lambda m: (m, 0))] * 2,
        out_shape=[jax.ShapeDtypeStruct((t, LANES), F32)] * 2,
        compiler_params=_params("parallel"),
        name="rope_tables",
    )(positions.reshape(t, 1), invf)


def _rope(x, cos, sin):
    lane = lax.broadcasted_iota(jnp.int32, x.shape, 1)
    rot = jnp.where(lane < ROPE_HALF, -pltpu.roll(x, LANES - ROPE_HALF, 1), pltpu.roll(x, ROPE_HALF, 1))
    return x * cos + rot * sin


def _modulate_into(h_ref, x_ref, sh_ref, sc_ref):
    @pl.when(pl.program_id(1) == 0)
    def _():
        h_ref[...] = (x_ref[...] * (1.0 + sc_ref[0]) + sh_ref[0]).astype(h_ref.dtype)


def _modmm_kernel(x_ref, sh_ref, sc_ref, w_ref, cs_ref, o_ref, h_ref):
    _modulate_into(h_ref, x_ref, sh_ref, sc_ref)
    acc = jnp.dot(h_ref[...], w_ref[...], preferred_element_type=F32)
    o_ref[...] = (acc * cs_ref[...]).astype(o_ref.dtype)


def modmm(x, modr, layer, nb, shift_chunk, w, col_scale, tm=1024, tn=1024):
    t, d = x.shape
    n = w.shape[1]
    tm, tn = _tile(t // nb, tm), _tile(n, tn)
    tpb = (t // nb) // tm
    return pl.pallas_call(
        _modmm_kernel,
        grid=(t // tm, n // tn),
        in_specs=[
            pl.BlockSpec((tm, d), lambda m, j: (m, 0)),
            _mod_spec(d, layer, nb, tpb, shift_chunk),
            _mod_spec(d, layer, nb, tpb, shift_chunk + 1),
            pl.BlockSpec((d, tn), lambda m, j: (0, j)),
            pl.BlockSpec((1, tn), lambda m, j: (0, j)),
        ],
        out_specs=pl.BlockSpec((tm, tn), lambda m, j: (m, j)),
        out_shape=jax.ShapeDtypeStruct((t, n), BF16),
        scratch_shapes=[pltpu.VMEM((tm, d), BF16)],
        compiler_params=_params("parallel", "arbitrary"),
        name="modmm",
    )(x, modr, modr, w, col_scale)


def _modmm_t_kernel(x_ref, sh_ref, sc_ref, wt_ref, o_ref, h_ref, *, bk):
    _modulate_into(h_ref, x_ref, sh_ref, sc_ref)
    res = lax.dot_general(wt_ref[...], h_ref[...], (((1,), (1,)), ((), ())), preferred_element_type=F32)
    for j in range(o_ref.shape[0]):
        o_ref[j] = res[:, j * bk:(j + 1) * bk].astype(o_ref.dtype)


def modmm_t(x, modr, layer, nb, shift_chunk, wt, bk, tm=1024, tn=1024):
    t, d = x.shape
    n = wt.shape[0]
    tm, tn = _tile(t // nb, tm, bk), _tile(n, tn)
    tpb = (t // nb) // tm
    return pl.pallas_call(
        functools.partial(_modmm_t_kernel, bk=bk),
        grid=(t // tm, n // tn),
        in_specs=[
            pl.BlockSpec((tm, d), lambda m, j: (m, 0)),
            _mod_spec(d, layer, nb, tpb, shift_chunk),
            _mod_spec(d, layer, nb, tpb, shift_chunk + 1),
            pl.BlockSpec((tn, d), lambda m, j: (j, 0)),
        ],
        out_specs=pl.BlockSpec((tm // bk, tn, bk), lambda m, j: (m, j, 0)),
        out_shape=jax.ShapeDtypeStruct((t // bk, n, bk), BF16),
        scratch_shapes=[pltpu.VMEM((tm, d), BF16)],
        compiler_params=_params("parallel", "arbitrary"),
        name="modmm_t",
    )(x, modr, modr, wt)


def _modmm_swiglu_kernel(x_ref, sh_ref, sc_ref, wg_ref, wu_ref, o_ref, h_ref):
    _modulate_into(h_ref, x_ref, sh_ref, sc_ref)
    h = h_ref[...]
    g = jnp.dot(h, wg_ref[...], preferred_element_type=F32)
    u = jnp.dot(h, wu_ref[...], preferred_element_type=F32)
    o_ref[...] = (g * jax.nn.sigmoid(g) * u).astype(o_ref.dtype)


def modmm_swiglu(x, modr, layer, nb, shift_chunk, w_gu, tm=1024, tn=512):
    t, d = x.shape
    f = w_gu.shape[1] // 2
    tm, tn = _tile(t // nb, tm), _tile(f, tn)
    tpb = (t // nb) // tm
    nf = f // tn
    return pl.pallas_call(
        _modmm_swiglu_kernel,
        grid=(t // tm, nf),
        in_specs=[
            pl.BlockSpec((tm, d), lambda m, j: (m, 0)),
            _mod_spec(d, layer, nb, tpb, shift_chunk),
            _mod_spec(d, layer, nb, tpb, shift_chunk + 1),
            pl.BlockSpec((d, tn), lambda m, j: (0, j)),
            pl.BlockSpec((d, tn), lambda m, j: (0, j + nf)),
        ],
        out_specs=pl.BlockSpec((tm, tn), lambda m, j: (m, j)),
        out_shape=jax.ShapeDtypeStruct((t, f), BF16),
        scratch_shapes=[pltpu.VMEM((tm, d), BF16)],
        compiler_params=_params("parallel", "arbitrary"),
        name="modmm_swiglu",
    )(x, modr, modr, w_gu, w_gu)


def _res_ln(x, y, gate, ln_g, ln_b, alpha):
    v = alpha * x + (1.0 + gate) * y
    mu = jnp.mean(v, axis=-1, keepdims=True)
    dv = v - mu
    var = jnp.mean(dv * dv, axis=-1, keepdims=True)
    return dv * lax.rsqrt(var + LN_EPS) * ln_g + ln_b


LN_ROW_CHUNKS = 4


def _mm_res_ln_kernel(a_ref, w_ref, x_ref, gate_ref, lng_ref, lnb_ref, o_ref, acc_ref, *, alpha, nk):
    k = pl.program_id(1)
    if nk > 1:
        @pl.when(k == 0)
        def _():
            acc_ref[...] = jnp.dot(a_ref[...], w_ref[...], preferred_element_type=F32)

        @pl.when(jnp.logical_and(k > 0, k < nk - 1))
        def _():
            acc_ref[...] += jnp.dot(a_ref[...], w_ref[...], preferred_element_type=F32)

    @pl.when(k == nk - 1)
    def _():
        chunks = LN_ROW_CHUNKS if a_ref.shape[0] % (8 * LN_ROW_CHUNKS) == 0 else 1
        rc = a_ref.shape[0] // chunks

        def finish(c, y):
            rows = slice(c * rc, (c + 1) * rc)
            o_ref[rows, :] = _res_ln(x_ref[rows, :], y, gate_ref[0], lng_ref[...], lnb_ref[...], alpha)

        pending = None
        for c in range(chunks):
            rows = slice(c * rc, (c + 1) * rc)
            y = jnp.dot(a_ref[rows, :], w_ref[...], preferred_element_type=F32)
            if nk > 1:
                y = y + acc_ref[rows, :]
            if pending is not None:
                finish(*pending)
            pending = (c, y)
        finish(*pending)


def mm_res_ln(a, w, x, modr, layer, nb, gate_chunk, ln_g, ln_b, alpha, tm=512, tk=2048):
    t, kdim = a.shape
    d = w.shape[1]
    tm, tk = _tile(t // nb, tm), _tile(kdim, tk)
    tpb = (t // nb) // tm
    nk = kdim // tk
    return pl.pallas_call(
        functools.partial(_mm_res_ln_kernel, alpha=alpha, nk=nk),
        grid=(t // tm, nk),
        in_specs=[
            pl.BlockSpec((tm, tk), lambda m, k: (m, k)),
            pl.BlockSpec((tk, d), lambda m, k: (k, 0)),
            pl.BlockSpec((tm, d), lambda m, k: (m, 0)),
            _mod_spec(d, layer, nb, tpb, gate_chunk),
            pl.BlockSpec((1, d), lambda m, k: (0, 0)),
            pl.BlockSpec((1, d), lambda m, k: (0, 0)),
        ],
        out_specs=pl.BlockSpec((tm, d), lambda m, k: (m, 0)),
        out_shape=jax.ShapeDtypeStruct((t, d), F32),
        scratch_shapes=[pltpu.VMEM((tm, d), F32)],
        compiler_params=_params("parallel", "arbitrary"),
        name="mm_res_ln",
    )(a, w, x, modr, ln_g.reshape(1, d), ln_b.reshape(1, d))


def _head_cols(h):
    return slice(h * HEAD_DIM, (h + 1) * HEAD_DIM)


def _sb_attn_kernel(q_ref, k_ref, vt_ref, o_ref, *, bk, nq, heads):
    tq = 2 * bk
    key = lax.broadcasted_iota(jnp.int32, (bk, tq), 0)
    qry = lax.broadcasted_iota(jnp.int32, (bk, tq), 1)
    r2 = lax.broadcasted_iota(jnp.int32, (bk, bk), 0)
    c2 = lax.broadcasted_iota(jnp.int32, (bk, bk), 1)
    later = (c2 > r2).astype(BF16)
    later2 = jnp.concatenate([later, later], axis=1)

    def blocks(qs, kbs, offsets, carries):
        hs = range(heads)
        units = [(h, i) for i in range(len(kbs)) for h in hs]
        masks = [None if off is None else key + off < qry for off in offsets]
        zs = [lax.dot_general(k_ref[pl.ds(pl.multiple_of(kbs[i] * bk, bk), bk), _head_cols(h)], qs[h],
                              (((1,), (1,)), ((), ())), preferred_element_type=F32) for h, i in units]
        bases, splits, sums = [], [], []
        for (h, i), z in zip(units, zs):
            neg_abs = pltpu.bitcast(pltpu.bitcast(z, jnp.uint32) | jnp.uint32(0x80000000), F32)
            sp = jnp.maximum(z, 0.0) + jnp.log(1.0 + jnp.exp2(neg_abs)) * LOG2_E
            bases.append(z - sp)
            if masks[i] is not None:
                sp = jnp.where(masks[i], sp, 0.0)
            hi = sp.astype(BF16)
            lo = (sp - hi.astype(F32)).astype(BF16)
            splits.append(jnp.concatenate([hi, lo], axis=0))
            sums.append(jnp.sum(sp, axis=0, keepdims=True))
        tails = [jnp.dot(later2, split, preferred_element_type=F32) for split in splits]
        runs = [carries[h][0] for h in hs]
        probs = []
        for u, (h, i) in enumerate(units):
            a = jnp.exp2(bases[u] - tails[u] - runs[h])
            if masks[i] is not None:
                a = jnp.where(masks[i], a, 0.0)
            probs.append(a.astype(BF16))
            runs[h] = runs[h] + sums[u]
        accs = [carries[h][1] for h in hs]
        for u, (h, i) in enumerate(units):
            accs[h] = accs[h] + jnp.dot(vt_ref[kbs[i], _head_cols(h), :], probs[u], preferred_element_type=F32)
        return tuple(zip(runs, accs))

    def qbody(qi, _):
        q0 = pl.multiple_of(qi * tq, tq)
        qs = [q_ref[pl.ds(q0, tq), _head_cols(h)] for h in range(heads)]
        carries = tuple((jnp.zeros((1, tq), F32), jnp.zeros((HEAD_DIM, tq), F32)) for _ in range(heads))
        carries = blocks(qs, [2 * qi + 1, 2 * qi], [bk, 0], carries)

        def kbody(i, carries):
            pair = qi - 1 - i
            return blocks(qs, [2 * pair + 1, 2 * pair], [None, None], carries)

        carries = lax.fori_loop(0, qi, kbody, carries)
        for h in range(heads):
            o_ref[pl.ds(q0, tq), _head_cols(h)] = carries[h][1].T.astype(o_ref.dtype)
        return 0

    lax.fori_loop(0, nq, qbody, 0)


def sb_attention(qk, vt, nb, heads=4):
    t, n2 = qk.shape
    s = t // nb
    nh = n2 // (2 * HEAD_DIM)
    bk = vt.shape[2]
    assert s % (2 * bk) == 0
    heads = min(heads, nh)
    ng = nh // heads
    cols = heads * HEAD_DIM
    return pl.pallas_call(
        functools.partial(_sb_attn_kernel, bk=bk, nq=s // (2 * bk), heads=heads),
        grid=(nb, ng),
        in_specs=[
            pl.BlockSpec((s, cols), lambda b, g: (b, g)),
            pl.BlockSpec((s, cols), lambda b, g: (b, ng + g)),
            pl.BlockSpec((s // bk, cols, bk), lambda b, g: (b, g, 0)),
        ],
        out_specs=pl.BlockSpec((s, cols), lambda b, g: (b, g)),
        out_shape=jax.ShapeDtypeStruct((t, nh * HEAD_DIM), BF16),
        compiler_params=_params("parallel", "parallel"),
        name="sb_attention",
    )(qk, qk, vt)


def _mla_in_kernel(x_ref, sh_ref, sc_ref, w_ref, qn_ref, kvn_ref, cos_ref, sin_ref,
                   cq_ref, ckv_ref, kr_ref, *, q_rank, kv_rank):
    h = (x_ref[...] * (1.0 + sc_ref[0]) + sh_ref[0]).astype(BF16)
    lat = jnp.dot(h, w_ref[...], preferred_element_type=F32)

    def rms(v, g):
        return v * lax.rsqrt(jnp.mean(v * v, axis=-1, keepdims=True) + RMS_EPS) * g

    cq_ref[...] = rms(lat[:, :q_rank], qn_ref[...]).astype(cq_ref.dtype)
    ckv_ref[...] = rms(lat[:, q_rank:q_rank + kv_rank], kvn_ref[...]).astype(ckv_ref.dtype)
    kr_ref[...] = _rope(lat[:, q_rank + kv_rank:], cos_ref[...], sin_ref[...]).astype(kr_ref.dtype)


def mla_in(x, modr, layer, nb, w_in_pad, q_norm, kv_norm, cos, sin, tm=512):
    t, d = x.shape
    q_rank, kv_rank = q_norm.shape[0], kv_norm.shape[0]
    n = w_in_pad.shape[1]
    assert n == q_rank + kv_rank + LANES
    tm = _tile(t // nb, tm)
    tpb = (t // nb) // tm
    return pl.pallas_call(
        functools.partial(_mla_in_kernel, q_rank=q_rank, kv_rank=kv_rank),
        grid=(t // tm,),
        in_specs=[
            pl.BlockSpec((tm, d), lambda m: (m, 0)),
            _mod_spec(d, layer, nb, tpb, 0),
            _mod_spec(d, layer, nb, tpb, 1),
            pl.BlockSpec((d, n), lambda m: (0, 0)),
            pl.BlockSpec((1, q_rank), lambda m: (0, 0)),
            pl.BlockSpec((1, kv_rank), lambda m: (0, 0)),
            pl.BlockSpec((tm, LANES), lambda m: (m, 0)),
            pl.BlockSpec((tm, LANES), lambda m: (m, 0)),
        ],
        out_specs=[
            pl.BlockSpec((tm, q_rank), lambda m: (m, 0)),
            pl.BlockSpec((tm, kv_rank), lambda m: (m, 0)),
            pl.BlockSpec((tm, LANES), lambda m: (m, 0)),
        ],
        out_shape=[
            jax.ShapeDtypeStruct((t, q_rank), BF16),
            jax.ShapeDtypeStruct((t, kv_rank), BF16),
            jax.ShapeDtypeStruct((t, LANES), BF16),
        ],
        compiler_params=_params("parallel"),
        name="mla_in",
    )(x, modr, modr, w_in_pad, q_norm.reshape(1, q_rank), kv_norm.reshape(1, kv_rank), cos, sin)


def _mm_kernel(a_ref, w_ref, o_ref):
    o_ref[...] = jnp.dot(a_ref[...], w_ref[...], preferred_element_type=F32).astype(o_ref.dtype)


def mm(a, w, tm=1024, tn=1024):
    t, k = a.shape
    n = w.shape[1]
    tm, tn = _tile(t, tm), _tile(n, tn)
    return pl.pallas_call(
        _mm_kernel,
        grid=(t // tm, n // tn),
        in_specs=[pl.BlockSpec((tm, k), lambda m, j: (m, 0)), pl.BlockSpec((k, tn), lambda m, j: (0, j))],
        out_specs=pl.BlockSpec((tm, tn), lambda m, j: (m, j)),
        out_shape=jax.ShapeDtypeStruct((t, n), BF16),
        compiler_params=_params("parallel", "parallel"),
        name="mm",
    )(a, w)


def _mm_q_rope_kernel(a_ref, w_ref, cos_ref, sin_ref, o_ref, *, scale, heads):
    acc = jnp.dot(a_ref[...], w_ref[...], preferred_element_type=F32)
    cos, sin = cos_ref[...], sin_ref[...]
    for h in range(heads):
        c0 = 2 * LANES * h
        o_ref[:, c0:c0 + LANES] = (acc[:, c0:c0 + LANES] * scale).astype(o_ref.dtype)
        roped = _rope(acc[:, c0 + LANES:c0 + 2 * LANES], cos, sin)
        o_ref[:, c0 + LANES:c0 + 2 * LANES] = (roped * scale).astype(o_ref.dtype)


def mm_q_rope(a, w, cos, sin, scale, tm=1024, heads_per_tile=4):
    t, k = a.shape
    n = w.shape[1]
    tm = _tile(t, tm)
    tn = _tile(n, 2 * LANES * heads_per_tile)
    return pl.pallas_call(
        functools.partial(_mm_q_rope_kernel, scale=scale, heads=tn // (2 * LANES)),
        grid=(t // tm, n // tn),
        in_specs=[
            pl.BlockSpec((tm, k), lambda m, j: (m, 0)),
            pl.BlockSpec((k, tn), lambda m, j: (0, j)),
            pl.BlockSpec((tm, LANES), lambda m, j: (m, 0)),
            pl.BlockSpec((tm, LANES), lambda m, j: (m, 0)),
        ],
        out_specs=pl.BlockSpec((tm, tn), lambda m, j: (m, j)),
        out_shape=jax.ShapeDtypeStruct((t, n), BF16),
        compiler_params=_params("parallel", "parallel"),
        name="mm_q_rope",
    )(a, w, cos, sin)


def _mm_t_kernel(a_ref, wt_ref, o_ref, *, bk):
    res = lax.dot_general(wt_ref[...], a_ref[...], (((1,), (1,)), ((), ())), preferred_element_type=F32)
    for j in range(o_ref.shape[0]):
        o_ref[j] = res[:, j * bk:(j + 1) * bk].astype(o_ref.dtype)


def mm_t(a, wt, bk, tm=1024, tn=1024):
    t, k = a.shape
    n = wt.shape[0]
    tm, tn = _tile(t, tm, bk), _tile(n, tn)
    return pl.pallas_call(
        functools.partial(_mm_t_kernel, bk=bk),
        grid=(t // tm, n // tn),
        in_specs=[pl.BlockSpec((tm, k), lambda m, j: (m, 0)), pl.BlockSpec((tn, k), lambda m, j: (j, 0))],
        out_specs=pl.BlockSpec((tm // bk, tn, bk), lambda m, j: (m, j, 0)),
        out_shape=jax.ShapeDtypeStruct((t // bk, n, bk), BF16),
        compiler_params=_params("parallel", "parallel"),
        name="mm_t",
    )(a, wt)


def _mla_attn_kernel(q_ref, kn_ref, kr_ref, vt_ref, o_ref, kcat_ref, *, tq, nq, heads):
    for h in range(heads):
        c0 = 2 * HEAD_DIM * h
        kcat_ref[:, c0:c0 + HEAD_DIM] = kn_ref[:, _head_cols(h)]
        kcat_ref[:, c0 + HEAD_DIM:c0 + 2 * HEAD_DIM] = kr_ref[...]
    key = lax.broadcasted_iota(jnp.int32, (tq, tq), 0)
    qry = lax.broadcasted_iota(jnp.int32, (tq, tq), 1)
    causal = key <= qry

    def blocks(qs, kb, carries, masked):
        hs = range(heads)
        k0 = pl.multiple_of(kb * tq, tq)
        ss = [lax.dot_general(kcat_ref[pl.ds(k0, tq), 2 * HEAD_DIM * h:2 * HEAD_DIM * (h + 1)], qs[h],
                              (((1,), (1,)), ((), ())), preferred_element_type=F32) for h in hs]
        if masked:
            ss = [jnp.where(causal, s, -jnp.inf) for s in ss]
        m_new, ps, corrs, ls = [], [], [], []
        for h in hs:
            m_old, l_old, _ = carries[h]
            m = jnp.maximum(m_old, jnp.max(ss[h], axis=0, keepdims=True))
            p = jnp.exp2(ss[h] - m)
            corr = jnp.exp2(m_old - m)
            m_new.append(m)
            corrs.append(corr)
            ls.append(l_old * corr + jnp.sum(p, axis=0, keepdims=True))
            ps.append(p.astype(BF16))
        accs = [carries[h][2] * corrs[h] + jnp.dot(vt_ref[kb, _head_cols(h), :], ps[h], preferred_element_type=F32)
                for h in hs]
        return tuple(zip(m_new, ls, accs))

    def qbody(qi, _):
        q0 = pl.multiple_of(qi * tq, tq)
        qs = [q_ref[pl.ds(q0, tq), 2 * HEAD_DIM * h:2 * HEAD_DIM * (h + 1)] for h in range(heads)]
        carries = tuple((jnp.full((1, tq), -jnp.inf, F32), jnp.zeros((1, tq), F32), jnp.zeros((HEAD_DIM, tq), F32))
                        for _ in range(heads))
        carries = blocks(qs, qi, carries, True)

        def kbody(i, carries):
            return blocks(qs, i, carries, False)

        carries = lax.fori_loop(0, qi, kbody, carries)
        for h in range(heads):
            _, l, acc = carries[h]
            o_ref[pl.ds(q0, tq), _head_cols(h)] = (acc / l).T.astype(o_ref.dtype)
        return 0

    lax.fori_loop(0, nq, qbody, 0)


def mla_attention(q, kn, kr, vt, nb, heads=4):
    t = q.shape[0]
    s = t // nb
    nh = q.shape[1] // (2 * HEAD_DIM)
    tq = vt.shape[2]
    heads = min(heads, nh)
    return pl.pallas_call(
        functools.partial(_mla_attn_kernel, tq=tq, nq=s // tq, heads=heads),
        grid=(nb, nh // heads),
        in_specs=[
            pl.BlockSpec((s, 2 * HEAD_DIM * heads), lambda b, g: (b, g)),
            pl.BlockSpec((s, HEAD_DIM * heads), lambda b, g: (b, g)),
            pl.BlockSpec((s, LANES), lambda b, g: (b, 0)),
            pl.BlockSpec((s // tq, HEAD_DIM * heads, tq), lambda b, g: (b, g, 0)),
        ],
        out_specs=pl.BlockSpec((s, HEAD_DIM * heads), lambda b, g: (b, g)),
        out_shape=jax.ShapeDtypeStruct((t, nh * HEAD_DIM), BF16),
        scratch_shapes=[pltpu.VMEM((s, 2 * HEAD_DIM * heads), BF16)],
        compiler_params=_params("parallel", "parallel"),
        name="mla_attention",
    )(q, kn, kr, vt)


def _router_kernel(x_ref, sh_ref, sc_ref, wr_ref, h_ref, idx_ref, gate_ref, cnt_ref, tri_ref, base_ref,
                   *, tm, n_exp):
    @pl.when(pl.program_id(0) == 0)
    def _():
        r = lax.broadcasted_iota(jnp.int32, (tm, tm), 0)
        c = lax.broadcasted_iota(jnp.int32, (tm, tm), 1)
        tri_ref[...] = (c <= r).astype(BF16)
        base_ref[...] = jnp.zeros_like(base_ref)

    h = x_ref[...] * (1.0 + sc_ref[0]) + sh_ref[0]
    half = h.shape[1] // 2
    ns = half // LANES
    bits = pltpu.bitcast(h, jnp.uint32)
    rne = bits + jnp.uint32(0x7FFF) + ((bits >> 16) & jnp.uint32(1))
    packed = (rne[:, :half] >> 16) | (rne[:, half:] & jnp.uint32(0xFFFF0000))
    for c in range(ns):
        h_ref[pl.ds(c, tm, stride=ns), :] = packed[:, c * LANES:(c + 1) * LANES]
    h_hi = h.astype(BF16)
    h_lo = (h - h_hi.astype(F32)).astype(BF16)
    w = wr_ref[...]
    w_hi = w.astype(BF16)
    w_lo = (w - w_hi.astype(F32)).astype(BF16)
    logits = (jnp.dot(h_hi, w_hi, preferred_element_type=F32)
              + jnp.dot(h_lo, w_hi, preferred_element_type=F32)
              + jnp.dot(h_hi, w_lo, preferred_element_type=F32))
    lane = lax.broadcasted_iota(jnp.int32, logits.shape, 1).astype(F32)
    lg = jnp.where(lane < n_exp, logits, -jnp.inf)
    m1 = jnp.max(lg, axis=-1, keepdims=True)
    i1 = jnp.min(jnp.where(lg == m1, lane, float(LANES)), axis=-1, keepdims=True)
    lg2 = jnp.where(lane == i1, -jnp.inf, lg)
    m2 = jnp.max(lg2, axis=-1, keepdims=True)
    i2 = jnp.min(jnp.where(lg2 == m2, lane, float(LANES)), axis=-1, keepdims=True)
    e = jnp.exp(m2 - m1)
    g1 = 1.0 / (1.0 + e)
    g2 = e * g1
    oh1 = lane == i1
    oh2 = lane == i2
    onehot = jnp.where(oh1 | oh2, 1.0, 0.0).astype(BF16)
    tot = base_ref[...] + jnp.dot(tri_ref[...], onehot, preferred_element_type=F32)
    r1 = jnp.sum(jnp.where(oh1, tot, 0.0), axis=-1, keepdims=True) - 1.0
    r2 = jnp.sum(jnp.where(oh2, tot, 0.0), axis=-1, keepdims=True) - 1.0
    last = tot[tm - 1:tm, :]
    base_ref[...] = last
    cnt_ref[...] = jnp.broadcast_to(last, cnt_ref.shape)
    info = jnp.where(lane == 0, i1, jnp.where(lane == 1, i2, jnp.where(lane == 2, r1, jnp.where(lane == 3, r2, 0.0))))
    idx_ref[...] = info.astype(jnp.int32)
    gate_ref[...] = jnp.where(lane == 0, g1, jnp.where(lane == 1, g2, 0.0))


def router(x, modr, layer, nb, w_router, tm=512):
    t, d = x.shape
    n_exp = w_router.shape[1]
    tm = _tile(t // nb, tm)
    tpb = (t // nb) // tm
    ns = d // (2 * LANES)
    wr = jnp.pad(w_router, ((0, 0), (0, LANES - n_exp)))
    return pl.pallas_call(
        functools.partial(_router_kernel, tm=tm, n_exp=n_exp),
        grid=(t // tm,),
        in_specs=[
            pl.BlockSpec((tm, d), lambda m: (m, 0)),
            _mod_spec(d, layer, nb, tpb, 3),
            _mod_spec(d, layer, nb, tpb, 4),
            pl.BlockSpec((d, LANES), lambda m: (0, 0)),
        ],
        out_specs=[
            pl.BlockSpec((tm * ns, LANES), lambda m: (m, 0)),
            pl.BlockSpec((tm, LANES), lambda m: (m, 0)),
            pl.BlockSpec((tm, LANES), lambda m: (m, 0)),
            pl.BlockSpec((8, LANES), lambda m: (0, 0)),
        ],
        out_shape=[
            jax.ShapeDtypeStruct((t * ns, LANES), jnp.uint32),
            jax.ShapeDtypeStruct((t, LANES), jnp.int32),
            jax.ShapeDtypeStruct((t, LANES), F32),
            jax.ShapeDtypeStruct((8, LANES), F32),
        ],
        scratch_shapes=[pltpu.VMEM((tm, tm), BF16), pltpu.VMEM((1, LANES), F32)],
        compiler_params=_params("arbitrary"),
        name="router",
    )(x, modr, modr, wr)


ROW_DMA_UNROLL = 8


def _slab(ref, row, ns):
    return ref.at[pl.ds(pl.multiple_of(row * ns, ns), ns), :]


def _gather_rows_kernel(idx_ref, idx_next_ref, src_ref, o_ref, buf_ref, sem, *, tg, ns, steps):
    i = pl.program_id(0)
    slot = i % 2

    def issue(ids_ref, s):
        def start(r, _):
            pltpu.make_async_copy(_slab(src_ref, ids_ref[0, 0, r], ns), _slab(buf_ref, s * tg + r, ns),
                                  sem.at[s]).start()
            return 0
        lax.fori_loop(0, tg, start, 0, unroll=ROW_DMA_UNROLL)

    @pl.when(i == 0)
    def _():
        issue(idx_ref, 0)

    @pl.when(i + 1 < steps)
    def _():
        issue(idx_next_ref, 1 - slot)

    base = pl.multiple_of(slot * (tg * ns), tg * ns)
    pltpu.make_async_copy(src_ref.at[pl.ds(0, tg * ns), :], buf_ref.at[pl.ds(base, tg * ns), :], sem.at[slot]).wait()
    half = ns * LANES
    for c in range(ns):
        words = buf_ref[pl.ds(base + c, tg, stride=ns), :]
        low = pltpu.bitcast(words << 16, F32)
        high = pltpu.bitcast(words & jnp.uint32(0xFFFF0000), F32)
        o_ref[:, c * LANES:(c + 1) * LANES] = low.astype(o_ref.dtype)
        o_ref[:, half + c * LANES:half + (c + 1) * LANES] = high.astype(o_ref.dtype)


def gather_rows(src, idx, ns, tg=512):
    n = idx.shape[0]
    tg = _tile(n, tg)
    steps = n // tg
    ids = idx.reshape(steps, 1, tg)
    return pl.pallas_call(
        functools.partial(_gather_rows_kernel, tg=tg, ns=ns, steps=steps),
        grid=(steps,),
        in_specs=[
            pl.BlockSpec((1, 1, tg), lambda m: (m, 0, 0), memory_space=pltpu.SMEM),
            pl.BlockSpec((1, 1, tg), lambda m: (jnp.minimum(m + 1, steps - 1), 0, 0), memory_space=pltpu.SMEM),
            pl.BlockSpec(memory_space=pl.ANY),
        ],
        out_specs=pl.BlockSpec((tg, 2 * ns * LANES), lambda m: (m, 0)),
        out_shape=jax.ShapeDtypeStruct((n, 2 * ns * LANES), BF16),
        scratch_shapes=[pltpu.VMEM((2 * tg * ns, LANES), src.dtype), pltpu.SemaphoreType.DMA((2,))],
        compiler_params=_params("arbitrary"),
        name="gather_rows",
    )(ids, ids, src)


def _gmm_swiglu_kernel(te_ref, nu_ref, a_ref, wg_ref, wu_ref, o_ref, wgb_ref, wub_ref):
    m = pl.program_id(1)
    used = m < nu_ref[0]
    fresh = jnp.logical_or(m == 0, te_ref[m] != te_ref[jnp.maximum(m - 1, 0)])

    @pl.when(fresh)
    def _():
        wgb_ref[...] = wg_ref[...].astype(BF16)
        wub_ref[...] = wu_ref[...].astype(BF16)

    @pl.when(used)
    def _():
        a = a_ref[...]
        g = jnp.dot(a, wgb_ref[...], preferred_element_type=F32)
        u = jnp.dot(a, wub_ref[...], preferred_element_type=F32)
        o_ref[...] = (g * jax.nn.sigmoid(g) * u).astype(o_ref.dtype)

    @pl.when(jnp.logical_not(used))
    def _():
        o_ref[...] = jnp.zeros_like(o_ref)


def gmm_swiglu(a, w_gu, layer, tile_expert, n_used, tm, tn=1024):
    ms, d = a.shape
    f = w_gu.shape[3] // 2
    tn = _tile(f, tn)
    nf = f // tn
    return pl.pallas_call(
        _gmm_swiglu_kernel,
        grid_spec=pltpu.PrefetchScalarGridSpec(
            num_scalar_prefetch=2,
            grid=(nf, ms // tm),
            in_specs=[
                pl.BlockSpec((tm, d), lambda j, m, te, nu: (m, 0)),
                pl.BlockSpec((None, None, d, tn), lambda j, m, te, nu: (layer, te[m], 0, j)),
                pl.BlockSpec((None, None, d, tn), lambda j, m, te, nu: (layer, te[m], 0, j + nf)),
            ],
            out_specs=pl.BlockSpec((tm, tn), lambda j, m, te, nu: (m, j)),
            scratch_shapes=[pltpu.VMEM((d, tn), BF16), pltpu.VMEM((d, tn), BF16)],
        ),
        out_shape=jax.ShapeDtypeStruct((ms, f), BF16),
        compiler_params=_params("parallel", "arbitrary"),
        name="gmm_swiglu",
    )(tile_expert, n_used, a, w_gu, w_gu)


def _gmm_kernel(te_ref, nu_ref, a_ref, w_ref, o_ref):
    used = pl.program_id(1) < nu_ref[0]

    @pl.when(used)
    def _():
        acc = jnp.dot(a_ref[...], w_ref[...], preferred_element_type=F32)
        for c in range(o_ref.shape[1]):
            o_ref[:, c, :] = acc[:, c * LANES:(c + 1) * LANES]

    @pl.when(jnp.logical_not(used))
    def _():
        o_ref[...] = jnp.zeros_like(o_ref)


def gmm(a, w, layer, tile_expert, n_used, tm, tn=1024):
    ms, k = a.shape
    n = w.shape[3]
    tn = _tile(n, tn, 8 * LANES)
    return pl.pallas_call(
        _gmm_kernel,
        grid_spec=pltpu.PrefetchScalarGridSpec(
            num_scalar_prefetch=2,
            grid=(n // tn, ms // tm),
            in_specs=[
                pl.BlockSpec((tm, k), lambda j, m, te, nu: (m, 0)),
                pl.BlockSpec((None, None, k, tn), lambda j, m, te, nu: (layer, te[m], 0, j)),
            ],
            out_specs=pl.BlockSpec((tm, tn // LANES, LANES), lambda j, m, te, nu: (m, j, 0)),
        ),
        out_shape=jax.ShapeDtypeStruct((ms, n // LANES, LANES), F32),
        compiler_params=_params("parallel", "arbitrary"),
        name="gmm",
    )(tile_expert, n_used, a, w)


def _combine_ln_kernel(p0_ref, p1_ref, p0_next_ref, p1_next_ref, y_ref, gate_ref, x_ref, gmod_ref, lng_ref, lnb_ref,
                       o_ref, ybuf_ref, ys_ref, sem, *, tm, ns, steps, alpha):
    i = pl.program_id(0)
    slot = i % 2

    def issue(pa_ref, pb_ref, s):
        def start(r, _):
            row = (s * TOP_K) * tm + r
            pltpu.make_async_copy(_slab(y_ref, pa_ref[0, 0, r], ns), _slab(ybuf_ref, row, ns), sem.at[s]).start()
            pltpu.make_async_copy(_slab(y_ref, pb_ref[0, 0, r], ns), _slab(ybuf_ref, row + tm, ns), sem.at[s]).start()
            return 0
        lax.fori_loop(0, tm, start, 0, unroll=ROW_DMA_UNROLL)

    @pl.when(i == 0)
    def _():
        issue(p0_ref, p1_ref, 0)

    @pl.when(i + 1 < steps)
    def _():
        issue(p0_next_ref, p1_next_ref, 1 - slot)

    rows = TOP_K * tm * ns
    base = pl.multiple_of(slot * rows, rows)
    pltpu.make_async_copy(y_ref.at[pl.ds(0, rows), :], ybuf_ref.at[pl.ds(base, rows), :], sem.at[slot]).wait()
    gates = gate_ref[...]
    g0, g1 = gates[:, 0:1], gates[:, 1:2]
    for c in range(ns):
        y0 = ybuf_ref[pl.ds(base + c, tm, stride=ns), :]
        y1 = ybuf_ref[pl.ds(base + tm * ns + c, tm, stride=ns), :]
        ys_ref[:, c * LANES:(c + 1) * LANES] = g0 * y0 + g1 * y1
    o_ref[...] = _res_ln(x_ref[...], ys_ref[...], gmod_ref[0], lng_ref[...], lnb_ref[...], alpha)


def combine_ln(y_slabs, pos0, pos1, gates, x, modr, layer, nb, ln_g, ln_b, alpha, tm=256):
    t, d = x.shape
    tm = _tile(t // nb, tm)
    tpb = (t // nb) // tm
    nt = t // tm
    ns = d // LANES
    ids_spec = pl.BlockSpec((1, 1, tm), lambda m: (m, 0, 0), memory_space=pltpu.SMEM)
    ids_next_spec = pl.BlockSpec((1, 1, tm), lambda m: (jnp.minimum(m + 1, nt - 1), 0, 0), memory_space=pltpu.SMEM)
    p0, p1 = pos0.reshape(nt, 1, tm), pos1.reshape(nt, 1, tm)
    return pl.pallas_call(
        functools.partial(_combine_ln_kernel, tm=tm, ns=ns, steps=nt, alpha=alpha),
        grid=(nt,),
        in_specs=[
            ids_spec,
            ids_spec,
            ids_next_spec,
            ids_next_spec,
            pl.BlockSpec(memory_space=pl.ANY),
            pl.BlockSpec((tm, LANES), lambda m: (m, 0)),
            pl.BlockSpec((tm, d), lambda m: (m, 0)),
            _mod_spec(d, layer, nb, tpb, 5),
            pl.BlockSpec((1, d), lambda m: (0, 0)),
            pl.BlockSpec((1, d), lambda m: (0, 0)),
        ],
        out_specs=pl.BlockSpec((tm, d), lambda m: (m, 0)),
        out_shape=jax.ShapeDtypeStruct((t, d), F32),
        scratch_shapes=[pltpu.VMEM((2 * TOP_K * tm * ns, LANES), F32), pltpu.VMEM((tm, d), F32),
                        pltpu.SemaphoreType.DMA((2,))],
        compiler_params=_params("arbitrary"),
        name="combine_ln",
    )(p0, p1, p0, p1, y_slabs.reshape(-1, LANES), gates, x, modr, ln_g.reshape(1, d), ln_b.reshape(1, d))


def moe_layer(x, modr, layer, nb, w_router, w_gu, w_down, moe_index, ln_g, ln_b, alpha, tm_g=512):
    t, d = x.shape
    n_exp = w_router.shape[1]
    h, info, gates, cnt = router(x, modr, layer, nb, w_router)
    counts = cnt[0, :n_exp].astype(jnp.int32)
    sizes = (counts + tm_g - 1) // tm_g * tm_g
    ends = jnp.cumsum(sizes)
    starts = ends - sizes
    pos0 = starts[info[:, 0]] + info[:, 2]
    pos1 = starts[info[:, 1]] + info[:, 3]
    ms = t * TOP_K + n_exp * tm_g
    tok = jnp.arange(t, dtype=jnp.int32)
    slot_token = jnp.zeros((ms,), jnp.int32).at[jnp.concatenate([pos0, pos1])].set(jnp.concatenate([tok, tok]))
    tile_start = jnp.arange(ms // tm_g, dtype=jnp.int32) * tm_g
    tile_expert = jnp.minimum(jnp.sum(tile_start[:, None] >= ends[None, :], axis=1), n_exp - 1).astype(jnp.int32)
    n_used = (ends[-1:] // tm_g).astype(jnp.int32)

    h_sorted = gather_rows(h, slot_token, d // (2 * LANES))
    act = gmm_swiglu(h_sorted, w_gu, moe_index, tile_expert, n_used, tm_g)
    y_slabs = gmm(act, w_down, moe_index, tile_expert, n_used, tm_g)
    return combine_ln(y_slabs, pos0, pos1, gates, x, modr, layer, nb, ln_g, ln_b, alpha)


def kernel(x, c, positions, mod_w, mod_b, ln_g, ln_b, sb_w_qkv, sb_w_o, mla_w_in, mla_q_norm, mla_kv_norm,
           mla_w_uq, mla_w_ukv, mla_w_o, ffn_w_gu, ffn_w_down, moe_router, moe_w_gu, moe_w_down):
    nb, s, d = x.shape
    depth = mod_w.shape[0]
    t = nb * s
    alpha = float((2 * depth) ** 0.25)
    xf = x.reshape(t, d)

    modr = mod_table(c, mod_w, mod_b)
    cos, sin = rope_tables(positions)

    sb_heads = sb_w_o.shape[1] // HEAD_DIM
    sb_scale = HEAD_DIM ** -0.5 * LOG2_E
    sb_col_scale = jnp.concatenate([jnp.full((sb_heads * HEAD_DIM,), sb_scale, F32),
                                    jnp.ones((2 * sb_heads * HEAD_DIM,), F32)]).reshape(1, -1)

    mla_heads = mla_w_o.shape[1] // HEAD_DIM
    q_rank = mla_q_norm.shape[1]
    mla_scale = float((HEAD_DIM + ROPE_DIM) ** -0.5) * LOG2_E
    moe_w_down_bf16 = moe_w_down.astype(BF16)

    for i in range(depth):
        j = i // 2
        if i % 2 == 0:
            n_qk = 2 * sb_heads * HEAD_DIM
            qk = modmm(xf, modr, i, nb, 0, sb_w_qkv[j, :, :n_qk].astype(BF16), sb_col_scale[:, :n_qk])
            vt = modmm_t(xf, modr, i, nb, 0, sb_w_qkv[j, :, n_qk:].T.astype(BF16), _tile(s, SB_KEY_BLOCK))
            o = sb_attention(qk, vt, nb)
            w_o = sb_w_o[j]
        else:
            w_in = jnp.pad(mla_w_in[j], ((0, 0), (0, LANES - ROPE_DIM))).astype(BF16)
            cq, ckv, kr = mla_in(xf, modr, i, nb, w_in, mla_q_norm[j], mla_kv_norm[j], cos, sin)
            w_uq = mla_w_uq[j].reshape(q_rank, mla_heads, HEAD_DIM + ROPE_DIM)
            w_uq = jnp.pad(w_uq, ((0, 0), (0, 0), (0, LANES - ROPE_DIM))).reshape(q_rank, mla_heads * 2 * LANES)
            q = mm_q_rope(cq, w_uq.astype(BF16), cos, sin, mla_scale)
            w_ukv = mla_w_ukv[j].reshape(-1, mla_heads, 2 * HEAD_DIM)
            w_uk = w_ukv[:, :, :HEAD_DIM].reshape(-1, mla_heads * HEAD_DIM).astype(BF16)
            w_uv_t = w_ukv[:, :, HEAD_DIM:].reshape(-1, mla_heads * HEAD_DIM).T.astype(BF16)
            kn = mm(ckv, w_uk)
            vt = mm_t(ckv, w_uv_t, _tile(s, ATTN_BLOCK))
            o = mla_attention(q, kn, kr, vt, nb)
            w_o = mla_w_o[j]
        xf = mm_res_ln(o, w_o.astype(BF16), xf, modr, i, nb, 2, ln_g[i, 0], ln_b[i, 0], alpha)
        if i % 2 == 0:
            act = modmm_swiglu(xf, modr, i, nb, 3, ffn_w_gu[j].astype(BF16))
            xf = mm_res_ln(act, ffn_w_down[j].astype(BF16), xf, modr, i, nb, 5, ln_g[i, 1], ln_b[i, 1], alpha)
        else:
            xf = moe_layer(xf, modr, i, nb, moe_router[j], moe_w_gu, moe_w_down_bf16, j,
                           ln_g[i, 1], ln_b[i, 1], alpha)
    return xf.reshape(nb, s, d)
```

```python
import functools

import jax
import jax.numpy as jnp
from jax import lax
from jax.experimental import pallas as pl
from jax.experimental.pallas import tpu as pltpu

F32 = jnp.float32
BF16 = jnp.bfloat16

HEAD_DIM = 128
ROPE_DIM = 64
ROPE_HALF = ROPE_DIM // 2
ROPE_THETA = 10000.0
N_MOD = 6
TOP_K = 2
LN_EPS = 1e-5
RMS_EPS = 1e-6
LOG2_E = 1.4426950408889634
ATTN_BLOCK = 512
SB_KEY_BLOCK = 256
LANES = 128
SUBLANES = 8
VMEM_LIMIT = 56 * 1024 * 1024


def _tile(n, want, align=LANES):
    if n <= want:
        return n
    t = want - want % align
    while n % t:
        t -= align
    return t


def _params(*sem):
    return pltpu.CompilerParams(dimension_semantics=sem, vmem_limit_bytes=VMEM_LIMIT)


def _mod_kernel(c_ref, w_ref, b_ref, o_ref):
    c = c_ref[...]
    cond = (c * jax.nn.sigmoid(c)).astype(BF16)
    o_ref[...] = jnp.dot(cond, w_ref[...].astype(BF16), preferred_element_type=F32) + b_ref[...]


def mod_table(c, mod_w, mod_b):
    depth, d, n = mod_w.shape
    b = c.shape[0]
    tn = _tile(n, 1024)
    out = pl.pallas_call(
        _mod_kernel,
        grid=(depth, n // tn),
        in_specs=[
            pl.BlockSpec((b, d), lambda i, j: (0, 0)),
            pl.BlockSpec((None, d, tn), lambda i, j: (i, 0, j)),
            pl.BlockSpec((None, 1, tn), lambda i, j: (i, 0, j)),
        ],
        out_specs=pl.BlockSpec((None, b, tn), lambda i, j: (i, 0, j)),
        out_shape=jax.ShapeDtypeStruct((depth, b, n), F32),
        compiler_params=_params("parallel", "parallel"),
        name="mod_table",
    )(c, mod_w, mod_b.reshape(depth, 1, n))
    return out.reshape(depth * b * N_MOD, 1, d)


def _mod_spec(d, layer, nb, tiles_per_batch, chunk):
    def index(m, *_):
        return ((layer * nb + m // tiles_per_batch) * N_MOD + chunk, 0, 0)
    return pl.BlockSpec((1, 1, d), index)


def _rope_table_kernel(pos_ref, invf_ref, cos_ref, sin_ref):
    ang = pos_ref[...].astype(F32) * invf_ref[...]
    lane = lax.broadcasted_iota(jnp.int32, ang.shape, 1)
    live = lane < ROPE_DIM
    cos_ref[...] = jnp.where(live, jnp.cos(ang), 0.0)
    sin_ref[...] = jnp.where(live, jnp.sin(ang), 0.0)


def rope_tables(positions):
    t = positions.size
    tm = _tile(t, 1024)
    inv_freq = ROPE_THETA ** (-jnp.arange(ROPE_HALF, dtype=F32) / ROPE_HALF)
    invf = jnp.concatenate([inv_freq, inv_freq, jnp.zeros((LANES - ROPE_DIM,), F32)]).reshape(1, LANES)
    return pl.pallas_call(
        _rope_table_kernel,
        grid=(t // tm,),
        in_specs=[pl.BlockSpec((tm, 1), lambda m: (m, 0)), pl.BlockSpec((1, LANES), lambda m: (0, 0))],
        out_specs=[pl.BlockSpec((tm, LANES), lambda m: (m, 0))] * 2,
        out_shape=[jax.ShapeDtypeStruct((t, LANES), F32)] * 2,
        compiler_params=_params("parallel"),
        name="rope_tables",
    )(positions.reshape(t, 1), invf)


def _rope(x, cos, sin):
    lane = lax.broadcasted_iota(jnp.int32, x.shape, 1)
    rot = jnp.where(lane < ROPE_HALF, -pltpu.roll(x, LANES - ROPE_HALF, 1), pltpu.roll(x, ROPE_HALF, 1))
    return x * cos + rot * sin


def _modulate_into(h_ref, x_ref, sh_ref, sc_ref):
    @pl.when(pl.program_id(1) == 0)
    def _():
        h_ref[...] = (x_ref[...] * (1.0 + sc_ref[0]) + sh_ref[0]).astype(h_ref.dtype)


def _modmm_kernel(x_ref, sh_ref, sc_ref, w_ref, cs_ref, o_ref, h_ref):
    _modulate_into(h_ref, x_ref, sh_ref, sc_ref)
    acc = jnp.dot(h_ref[...], w_ref[...], preferred_element_type=F32)
    o_ref[...] = (acc * cs_ref[...]).astype(o_ref.dtype)


def modmm(x, modr, layer, nb, shift_chunk, w, col_scale, tm=1024, tn=1024):
    t, d = x.shape
    n = w.shape[1]
    tm, tn = _tile(t // nb, tm), _tile(n, tn)
    tpb = (t // nb) // tm
    return pl.pallas_call(
        _modmm_kernel,
        grid=(t // tm, n // tn),
        in_specs=[
            pl.BlockSpec((tm, d), lambda m, j: (m, 0)),
            _mod_spec(d, layer, nb, tpb, shift_chunk),
            _mod_spec(d, layer, nb, tpb, shift_chunk + 1),
            pl.BlockSpec((d, tn), lambda m, j: (0, j)),
            pl.BlockSpec((1, tn), lambda m, j: (0, j)),
        ],
        out_specs=pl.BlockSpec((tm, tn), lambda m, j: (m, j)),
        out_shape=jax.ShapeDtypeStruct((t, n), BF16),
        scratch_shapes=[pltpu.VMEM((tm, d), BF16)],
        compiler_params=_params("parallel", "arbitrary"),
        name="modmm",
    )(x, modr, modr, w, col_scale)


def _modmm_t_kernel(x_ref, sh_ref, sc_ref, wt_ref, o_ref, h_ref, *, bk):
    _modulate_into(h_ref, x_ref, sh_ref, sc_ref)
    res = lax.dot_general(wt_ref[...], h_ref[...], (((1,), (1,)), ((), ())), preferred_element_type=F32)
    for j in range(o_ref.shape[0]):
        o_ref[j] = res[:, j * bk:(j + 1) * bk].astype(o_ref.dtype)


def modmm_t(x, modr, layer, nb, shift_chunk, wt, bk, tm=1024, tn=1024):
    t, d = x.shape
    n = wt.shape[0]
    tm, tn = _tile(t // nb, tm, bk), _tile(n, tn)
    tpb = (t // nb) // tm
    return pl.pallas_call(
        functools.partial(_modmm_t_kernel, bk=bk),
        grid=(t // tm, n // tn),
        in_specs=[
            pl.BlockSpec((tm, d), lambda m, j: (m, 0)),
            _mod_spec(d, layer, nb, tpb, shift_chunk),
            _mod_spec(d, layer, nb, tpb, shift_chunk + 1),
            pl.BlockSpec((tn, d), lambda m, j: (j, 0)),
        ],
        out_specs=pl.BlockSpec((tm // bk, tn, bk), lambda m, j: (m, j, 0)),
        out_shape=jax.ShapeDtypeStruct((t // bk, n, bk), BF16),
        scratch_shapes=[pltpu.VMEM((tm, d), BF16)],
        compiler_params=_params("parallel", "arbitrary"),
        name="modmm_t",
    )(x, modr, modr, wt)


def _modmm_swiglu_kernel(x_ref, sh_ref, sc_ref, wg_ref, wu_ref, o_ref, h_ref):
    _modulate_into(h_ref, x_ref, sh_ref, sc_ref)
    h = h_ref[...]
    g = jnp.dot(h, wg_ref[...], preferred_element_type=F32)
    u = jnp.dot(h, wu_ref[...], preferred_element_type=F32)
    o_ref[...] = (g * jax.nn.sigmoid(g) * u).astype(o_ref.dtype)


def modmm_swiglu(x, modr, layer, nb, shift_chunk, w_gu, tm=1024, tn=512):
    t, d = x.shape
    f = w_gu.shape[1] // 2
    tm, tn = _tile(t // nb, tm), _tile(f, tn)
    tpb = (t // nb) // tm
    nf = f // tn
    return pl.pallas_call(
        _modmm_swiglu_kernel,
        grid=(t // tm, nf),
        in_specs=[
            pl.BlockSpec((tm, d), lambda m, j: (m, 0)),
            _mod_spec(d, layer, nb, tpb, shift_chunk),
            _mod_spec(d, layer, nb, tpb, shift_chunk + 1),
            pl.BlockSpec((d, tn), lambda m, j: (0, j)),
            pl.BlockSpec((d, tn), lambda m, j: (0, j + nf)),
        ],
        out_specs=pl.BlockSpec((tm, tn), lambda m, j: (m, j)),
        out_shape=jax.ShapeDtypeStruct((t, f), BF16),
        scratch_shapes=[pltpu.VMEM((tm, d), BF16)],
        compiler_params=_params("parallel", "arbitrary"),
        name="modmm_swiglu",
    )(x, modr, modr, w_gu, w_gu)


def _res_ln(x, y, gate, ln_g, ln_b, alpha):
    v = alpha * x + (1.0 + gate) * y
    mu = jnp.mean(v, axis=-1, keepdims=True)
    dv = v - mu
    var = jnp.mean(dv * dv, axis=-1, keepdims=True)
    return dv * lax.rsqrt(var + LN_EPS) * ln_g + ln_b


LN_ROW_CHUNKS = 4


def _mm_res_ln_kernel(a_ref, w_ref, x_ref, gate_ref, lng_ref, lnb_ref, o_ref, acc_ref, *, alpha, nk):
    k = pl.program_id(1)
    if nk > 1:
        @pl.when(k == 0)
        def _():
            acc_ref[...] = jnp.dot(a_ref[...], w_ref[...], preferred_element_type=F32)

        @pl.when(jnp.logical_and(k > 0, k < nk - 1))
        def _():
            acc_ref[...] += jnp.dot(a_ref[...], w_ref[...], preferred_element_type=F32)

    @pl.when(k == nk - 1)
    def _():
        chunks = LN_ROW_CHUNKS if a_ref.shape[0] % (SUBLANES * LN_ROW_CHUNKS) == 0 else 1
        rc = a_ref.shape[0] // chunks

        def finish(c, y):
            rows = slice(c * rc, (c + 1) * rc)
            o_ref[rows, :] = _res_ln(x_ref[rows, :], y, gate_ref[0], lng_ref[...], lnb_ref[...], alpha)

        pending = None
        for c in range(chunks):
            rows = slice(c * rc, (c + 1) * rc)
            y = jnp.dot(a_ref[rows, :], w_ref[...], preferred_element_type=F32)
            if nk > 1:
                y = y + acc_ref[rows, :]
            if pending is not None:
                finish(*pending)
            pending = (c, y)
        finish(*pending)


def mm_res_ln(a, w, x, modr, layer, nb, gate_chunk, ln_g, ln_b, alpha, tm=512, tk=2048):
    t, kdim = a.shape
    d = w.shape[1]
    tm, tk = _tile(t // nb, tm), _tile(kdim, tk)
    tpb = (t // nb) // tm
    nk = kdim // tk
    return pl.pallas_call(
        functools.partial(_mm_res_ln_kernel, alpha=alpha, nk=nk),
        grid=(t // tm, nk),
        in_specs=[
            pl.BlockSpec((tm, tk), lambda m, k: (m, k)),
            pl.BlockSpec((tk, d), lambda m, k: (k, 0)),
            pl.BlockSpec((tm, d), lambda m, k: (m, 0)),
            _mod_spec(d, layer, nb, tpb, gate_chunk),
            pl.BlockSpec((1, d), lambda m, k: (0, 0)),
            pl.BlockSpec((1, d), lambda m, k: (0, 0)),
        ],
        out_specs=pl.BlockSpec((tm, d), lambda m, k: (m, 0)),
        out_shape=jax.ShapeDtypeStruct((t, d), F32),
        scratch_shapes=[pltpu.VMEM((tm, d), F32)],
        compiler_params=_params("parallel", "arbitrary"),
        name="mm_res_ln",
    )(a, w, x, modr, ln_g.reshape(1, d), ln_b.reshape(1, d))


def _head_cols(h):
    return slice(h * HEAD_DIM, (h + 1) * HEAD_DIM)


def _sb_attn_kernel(q_ref, k_ref, vt_ref, o_ref, *, bk, nq, heads):
    tq = 2 * bk
    key = lax.broadcasted_iota(jnp.int32, (bk, tq), 0)
    qry = lax.broadcasted_iota(jnp.int32, (bk, tq), 1)
    r2 = lax.broadcasted_iota(jnp.int32, (bk, bk), 0)
    c2 = lax.broadcasted_iota(jnp.int32, (bk, bk), 1)
    later = (c2 > r2).astype(BF16)
    later2 = jnp.concatenate([later, later], axis=1)

    def blocks(qs, kbs, offsets, carries):
        hs = range(heads)
        units = [(h, i) for i in range(len(kbs)) for h in hs]
        masks = [None if off is None else key + off < qry for off in offsets]
        zs = [lax.dot_general(k_ref[pl.ds(pl.multiple_of(kbs[i] * bk, bk), bk), _head_cols(h)], qs[h],
                              (((1,), (1,)), ((), ())), preferred_element_type=F32) for h, i in units]
        bases, splits, sums = [], [], []
        for (h, i), z in zip(units, zs):
            neg_abs = pltpu.bitcast(pltpu.bitcast(z, jnp.uint32) | jnp.uint32(0x80000000), F32)
            sp = jnp.maximum(z, 0.0) + jnp.log(1.0 + jnp.exp2(neg_abs)) * LOG2_E
            bases.append(z - sp)
            if masks[i] is not None:
                sp = jnp.where(masks[i], sp, 0.0)
            hi = sp.astype(BF16)
            lo = (sp - hi.astype(F32)).astype(BF16)
            splits.append(jnp.concatenate([hi, lo], axis=0))
            sums.append(jnp.sum(sp, axis=0, keepdims=True))
        tails = [jnp.dot(later2, split, preferred_element_type=F32) for split in splits]
        runs = [carries[h][0] for h in hs]
        probs = []
        for u, (h, i) in enumerate(units):
            a = jnp.exp2(bases[u] - tails[u] - runs[h])
            if masks[i] is not None:
                a = jnp.where(masks[i], a, 0.0)
            probs.append(a.astype(BF16))
            runs[h] = runs[h] + sums[u]
        accs = [carries[h][1] for h in hs]
        for u, (h, i) in enumerate(units):
            accs[h] = accs[h] + jnp.dot(vt_ref[kbs[i], _head_cols(h), :], probs[u], preferred_element_type=F32)
        return tuple(zip(runs, accs))

    def qbody(qi, _):
        q0 = pl.multiple_of(qi * tq, tq)
        qs = [q_ref[pl.ds(q0, tq), _head_cols(h)] for h in range(heads)]
        carries = tuple((jnp.zeros((1, tq), F32), jnp.zeros((HEAD_DIM, tq), F32)) for _ in range(heads))
        carries = blocks(qs, [2 * qi + 1, 2 * qi], [bk, 0], carries)

        def kbody(i, carries):
            pair = qi - 1 - i
            return blocks(qs, [2 * pair + 1, 2 * pair], [None, None], carries)

        carries = lax.fori_loop(0, qi, kbody, carries)
        for h in range(heads):
            o_ref[pl.ds(q0, tq), _head_cols(h)] = carries[h][1].T.astype(o_ref.dtype)
        return 0

    lax.fori_loop(0, nq, qbody, 0)


def sb_attention(qk, vt, nb, heads=4):
    t, n2 = qk.shape
    s = t // nb
    nh = n2 // (2 * HEAD_DIM)
    bk = vt.shape[2]
    assert s % (2 * bk) == 0
    heads = min(heads, nh)
    ng = nh // heads
    cols = heads * HEAD_DIM
    return pl.pallas_call(
        functools.partial(_sb_attn_kernel, bk=bk, nq=s // (2 * bk), heads=heads),
        grid=(nb, ng),
        in_specs=[
            pl.BlockSpec((s, cols), lambda b, g: (b, g)),
            pl.BlockSpec((s, cols), lambda b, g: (b, ng + g)),
            pl.BlockSpec((s // bk, cols, bk), lambda b, g: (b, g, 0)),
        ],
        out_specs=pl.BlockSpec((s, cols), lambda b, g: (b, g)),
        out_shape=jax.ShapeDtypeStruct((t, nh * HEAD_DIM), BF16),
        compiler_params=_params("parallel", "parallel"),
        name="sb_attention",
    )(qk, qk, vt)


def _mla_in_kernel(x_ref, sh_ref, sc_ref, w_ref, qn_ref, kvn_ref, cos_ref, sin_ref,
                   cq_ref, ckv_ref, kr_ref, *, q_rank, kv_rank):
    h = (x_ref[...] * (1.0 + sc_ref[0]) + sh_ref[0]).astype(BF16)
    lat = jnp.dot(h, w_ref[...], preferred_element_type=F32)

    def rms(v, g):
        return v * lax.rsqrt(jnp.mean(v * v, axis=-1, keepdims=True) + RMS_EPS) * g

    cq_ref[...] = rms(lat[:, :q_rank], qn_ref[...]).astype(cq_ref.dtype)
    ckv_ref[...] = rms(lat[:, q_rank:q_rank + kv_rank], kvn_ref[...]).astype(ckv_ref.dtype)
    kr_ref[...] = _rope(lat[:, q_rank + kv_rank:], cos_ref[...], sin_ref[...]).astype(kr_ref.dtype)


def mla_in(x, modr, layer, nb, w_in_pad, q_norm, kv_norm, cos, sin, tm=512):
    t, d = x.shape
    q_rank, kv_rank = q_norm.shape[0], kv_norm.shape[0]
    n = w_in_pad.shape[1]
    assert n == q_rank + kv_rank + LANES
    tm = _tile(t // nb, tm)
    tpb = (t // nb) // tm
    return pl.pallas_call(
        functools.partial(_mla_in_kernel, q_rank=q_rank, kv_rank=kv_rank),
        grid=(t // tm,),
        in_specs=[
            pl.BlockSpec((tm, d), lambda m: (m, 0)),
            _mod_spec(d, layer, nb, tpb, 0),
            _mod_spec(d, layer, nb, tpb, 1),
            pl.BlockSpec((d, n), lambda m: (0, 0)),
            pl.BlockSpec((1, q_rank), lambda m: (0, 0)),
            pl.BlockSpec((1, kv_rank), lambda m: (0, 0)),
            pl.BlockSpec((tm, LANES), lambda m: (m, 0)),
            pl.BlockSpec((tm, LANES), lambda m: (m, 0)),
        ],
        out_specs=[
            pl.BlockSpec((tm, q_rank), lambda m: (m, 0)),
            pl.BlockSpec((tm, kv_rank), lambda m: (m, 0)),
            pl.BlockSpec((tm, LANES), lambda m: (m, 0)),
        ],
        out_shape=[
            jax.ShapeDtypeStruct((t, q_rank), BF16),
            jax.ShapeDtypeStruct((t, kv_rank), BF16),
            jax.ShapeDtypeStruct((t, LANES), BF16),
        ],
        compiler_params=_params("parallel"),
        name="mla_in",
    )(x, modr, modr, w_in_pad, q_norm.reshape(1, q_rank), kv_norm.reshape(1, kv_rank), cos, sin)


def _mm_kernel(a_ref, w_ref, o_ref):
    o_ref[...] = jnp.dot(a_ref[...], w_ref[...], preferred_element_type=F32).astype(o_ref.dtype)


def mm(a, w, tm=1024, tn=1024):
    t, k = a.shape
    n = w.shape[1]
    tm, tn = _tile(t, tm), _tile(n, tn)
    return pl.pallas_call(
        _mm_kernel,
        grid=(t // tm, n // tn),
        in_specs=[pl.BlockSpec((tm, k), lambda m, j: (m, 0)), pl.BlockSpec((k, tn), lambda m, j: (0, j))],
        out_specs=pl.BlockSpec((tm, tn), lambda m, j: (m, j)),
        out_shape=jax.ShapeDtypeStruct((t, n), BF16),
        compiler_params=_params("parallel", "parallel"),
        name="mm",
    )(a, w)


def _mm_q_rope_kernel(a_ref, w_ref, cos_ref, sin_ref, o_ref, *, scale, heads):
    acc = jnp.dot(a_ref[...], w_ref[...], preferred_element_type=F32)
    cos, sin = cos_ref[...], sin_ref[...]
    for h in range(heads):
        c0 = 2 * LANES * h
        o_ref[:, c0:c0 + LANES] = (acc[:, c0:c0 + LANES] * scale).astype(o_ref.dtype)
        roped = _rope(acc[:, c0 + LANES:c0 + 2 * LANES], cos, sin)
        o_ref[:, c0 + LANES:c0 + 2 * LANES] = (roped * scale).astype(o_ref.dtype)


def mm_q_rope(a, w, cos, sin, scale, tm=1024, heads_per_tile=4):
    t, k = a.shape
    n = w.shape[1]
    tm = _tile(t, tm)
    tn = _tile(n, 2 * LANES * heads_per_tile)
    return pl.pallas_call(
        functools.partial(_mm_q_rope_kernel, scale=scale, heads=tn // (2 * LANES)),
        grid=(t // tm, n // tn),
        in_specs=[
            pl.BlockSpec((tm, k), lambda m, j: (m, 0)),
            pl.BlockSpec((k, tn), lambda m, j: (0, j)),
            pl.BlockSpec((tm, LANES), lambda m, j: (m, 0)),
            pl.BlockSpec((tm, LANES), lambda m, j: (m, 0)),
        ],
        out_specs=pl.BlockSpec((tm, tn), lambda m, j: (m, j)),
        out_shape=jax.ShapeDtypeStruct((t, n), BF16),
        compiler_params=_params("parallel", "parallel"),
        name="mm_q_rope",
    )(a, w, cos, sin)


def _mm_t_kernel(a_ref, wt_ref, o_ref, *, bk):
    res = lax.dot_general(wt_ref[...], a_ref[...], (((1,), (1,)), ((), ())), preferred_element_type=F32)
    for j in range(o_ref.shape[0]):
        o_ref[j] = res[:, j * bk:(j + 1) * bk].astype(o_ref.dtype)


def mm_t(a, wt, bk, tm=1024, tn=1024):
    t, k = a.shape
    n = wt.shape[0]
    tm, tn = _tile(t, tm, bk), _tile(n, tn)
    return pl.pallas_call(
        functools.partial(_mm_t_kernel, bk=bk),
        grid=(t // tm, n // tn),
        in_specs=[pl.BlockSpec((tm, k), lambda m, j: (m, 0)), pl.BlockSpec((tn, k), lambda m, j: (j, 0))],
        out_specs=pl.BlockSpec((tm // bk, tn, bk), lambda m, j: (m, j, 0)),
        out_shape=jax.ShapeDtypeStruct((t // bk, n, bk), BF16),
        compiler_params=_params("parallel", "parallel"),
        name="mm_t",
    )(a, wt)


def _mla_attn_kernel(q_ref, kn_ref, kr_ref, vt_ref, o_ref, kcat_ref, *, tq, nq, heads):
    for h in range(heads):
        c0 = 2 * HEAD_DIM * h
        kcat_ref[:, c0:c0 + HEAD_DIM] = kn_ref[:, _head_cols(h)]
        kcat_ref[:, c0 + HEAD_DIM:c0 + 2 * HEAD_DIM] = kr_ref[...]
    key = lax.broadcasted_iota(jnp.int32, (tq, tq), 0)
    qry = lax.broadcasted_iota(jnp.int32, (tq, tq), 1)
    causal = key <= qry

    def blocks(qs, kb, carries, masked):
        hs = range(heads)
        k0 = pl.multiple_of(kb * tq, tq)
        ss = [lax.dot_general(kcat_ref[pl.ds(k0, tq), 2 * HEAD_DIM * h:2 * HEAD_DIM * (h + 1)], qs[h],
                              (((1,), (1,)), ((), ())), preferred_element_type=F32) for h in hs]
        if masked:
            ss = [jnp.where(causal, s, -jnp.inf) for s in ss]
        m_new, ps, corrs, ls = [], [], [], []
        for h in hs:
            m_old, l_old, _ = carries[h]
            m = jnp.maximum(m_old, jnp.max(ss[h], axis=0, keepdims=True))
            p = jnp.exp2(ss[h] - m)
            corr = jnp.exp2(m_old - m)
            m_new.append(m)
            corrs.append(corr)
            ls.append(l_old * corr + jnp.sum(p, axis=0, keepdims=True))
            ps.append(p.astype(BF16))
        accs = [carries[h][2] * corrs[h] + jnp.dot(vt_ref[kb, _head_cols(h), :], ps[h], preferred_element_type=F32)
                for h in hs]
        return tuple(zip(m_new, ls, accs))

    def qbody(qi, _):
        q0 = pl.multiple_of(qi * tq, tq)
        qs = [q_ref[pl.ds(q0, tq), 2 * HEAD_DIM * h:2 * HEAD_DIM * (h + 1)] for h in range(heads)]
        carries = tuple((jnp.full((1, tq), -jnp.inf, F32), jnp.zeros((1, tq), F32), jnp.zeros((HEAD_DIM, tq), F32))
                        for _ in range(heads))
        carries = blocks(qs, qi, carries, True)

        def kbody(i, carries):
            return blocks(qs, i, carries, False)

        carries = lax.fori_loop(0, qi, kbody, carries)
        for h in range(heads):
            _, l, acc = carries[h]
            o_ref[pl.ds(q0, tq), _head_cols(h)] = (acc / l).T.astype(o_ref.dtype)
        return 0

    lax.fori_loop(0, nq, qbody, 0)


def mla_attention(q, kn, kr, vt, nb, heads=4):
    t = q.shape[0]
    s = t // nb
    nh = q.shape[1] // (2 * HEAD_DIM)
    tq = vt.shape[2]
    heads = min(heads, nh)
    return pl.pallas_call(
        functools.partial(_mla_attn_kernel, tq=tq, nq=s // tq, heads=heads),
        grid=(nb, nh // heads),
        in_specs=[
            pl.BlockSpec((s, 2 * HEAD_DIM * heads), lambda b, g: (b, g)),
            pl.BlockSpec((s, HEAD_DIM * heads), lambda b, g: (b, g)),
            pl.BlockSpec((s, LANES), lambda b, g: (b, 0)),
            pl.BlockSpec((s // tq, HEAD_DIM * heads, tq), lambda b, g: (b, g, 0)),
        ],
        out_specs=pl.BlockSpec((s, HEAD_DIM * heads), lambda b, g: (b, g)),
        out_shape=jax.ShapeDtypeStruct((t, nh * HEAD_DIM), BF16),
        scratch_shapes=[pltpu.VMEM((s, 2 * HEAD_DIM * heads), BF16)],
        compiler_params=_params("parallel", "parallel"),
        name="mla_attention",
    )(q, kn, kr, vt)


def _router_kernel(x_ref, sh_ref, sc_ref, wr_ref, h_ref, idx_ref, gate_ref, cnt_ref, tri_ref, base_ref,
                   *, tm, n_exp):
    @pl.when(pl.program_id(0) == 0)
    def _():
        r = lax.broadcasted_iota(jnp.int32, (tm, tm), 0)
        c = lax.broadcasted_iota(jnp.int32, (tm, tm), 1)
        tri_ref[...] = (c <= r).astype(BF16)
        base_ref[...] = jnp.zeros_like(base_ref)

    h = x_ref[...] * (1.0 + sc_ref[0]) + sh_ref[0]
    half = h.shape[1] // 2
    ns = half // LANES
    bits = pltpu.bitcast(h, jnp.uint32)
    rne = bits + jnp.uint32(0x7FFF) + ((bits >> 16) & jnp.uint32(1))
    packed = (rne[:, :half] >> 16) | (rne[:, half:] & jnp.uint32(0xFFFF0000))
    for c in range(ns):
        h_ref[pl.ds(c, tm, stride=ns), :] = packed[:, c * LANES:(c + 1) * LANES]
    h_hi = h.astype(BF16)
    h_lo = (h - h_hi.astype(F32)).astype(BF16)
    w = wr_ref[...]
    w_hi = w.astype(BF16)
    w_lo = (w - w_hi.astype(F32)).astype(BF16)
    logits = (jnp.dot(h_hi, w_hi, preferred_element_type=F32)
              + jnp.dot(h_lo, w_hi, preferred_element_type=F32)
              + jnp.dot(h_hi, w_lo, preferred_element_type=F32))
    lane = lax.broadcasted_iota(jnp.int32, logits.shape, 1).astype(F32)
    lg = jnp.where(lane < n_exp, logits, -jnp.inf)
    m1 = jnp.max(lg, axis=-1, keepdims=True)
    i1 = jnp.min(jnp.where(lg == m1, lane, float(LANES)), axis=-1, keepdims=True)
    lg2 = jnp.where(lane == i1, -jnp.inf, lg)
    m2 = jnp.max(lg2, axis=-1, keepdims=True)
    i2 = jnp.min(jnp.where(lg2 == m2, lane, float(LANES)), axis=-1, keepdims=True)
    e = jnp.exp(m2 - m1)
    g1 = 1.0 / (1.0 + e)
    g2 = e * g1
    oh1 = lane == i1
    oh2 = lane == i2
    onehot = jnp.where(oh1 | oh2, 1.0, 0.0).astype(BF16)
    tot = base_ref[...] + jnp.dot(tri_ref[...], onehot, preferred_element_type=F32)
    r1 = jnp.sum(jnp.where(oh1, tot, 0.0), axis=-1, keepdims=True) - 1.0
    r2 = jnp.sum(jnp.where(oh2, tot, 0.0), axis=-1, keepdims=True) - 1.0
    last = tot[tm - 1:tm, :]
    base_ref[...] = last
    cnt_ref[...] = jnp.broadcast_to(last, cnt_ref.shape)
    info = jnp.where(lane == 0, i1, jnp.where(lane == 1, i2, jnp.where(lane == 2, r1, jnp.where(lane == 3, r2, 0.0))))
    idx_ref[...] = info.astype(jnp.int32)
    gate_ref[...] = jnp.where(lane == 0, g1, jnp.where(lane == 1, g2, 0.0))


def router(x, modr, layer, nb, w_router, tm=512):
    t, d = x.shape
    n_exp = w_router.shape[1]
    tm = _tile(t // nb, tm)
    tpb = (t // nb) // tm
    ns = d // (2 * LANES)
    wr = jnp.pad(w_router, ((0, 0), (0, LANES - n_exp)))
    return pl.pallas_call(
        functools.partial(_router_kernel, tm=tm, n_exp=n_exp),
        grid=(t // tm,),
        in_specs=[
            pl.BlockSpec((tm, d), lambda m: (m, 0)),
            _mod_spec(d, layer, nb, tpb, 3),
            _mod_spec(d, layer, nb, tpb, 4),
            pl.BlockSpec((d, LANES), lambda m: (0, 0)),
        ],
        out_specs=[
            pl.BlockSpec((tm * ns, LANES), lambda m: (m, 0)),
            pl.BlockSpec((tm, LANES), lambda m: (m, 0)),
            pl.BlockSpec((tm, LANES), lambda m: (m, 0)),
            pl.BlockSpec((SUBLANES, LANES), lambda m: (0, 0)),
        ],
        out_shape=[
            jax.ShapeDtypeStruct((t * ns, LANES), jnp.uint32),
            jax.ShapeDtypeStruct((t, LANES), jnp.int32),
            jax.ShapeDtypeStruct((t, LANES), F32),
            jax.ShapeDtypeStruct((SUBLANES, LANES), F32),
        ],
        scratch_shapes=[pltpu.VMEM((tm, tm), BF16), pltpu.VMEM((1, LANES), F32)],
        compiler_params=_params("arbitrary"),
        name="router",
    )(x, modr, modr, wr)


ROW_DMA_UNROLL = 8


def _slab(ref, row, ns):
    return ref.at[pl.ds(pl.multiple_of(row * ns, ns), ns), :]


def _gather_rows_kernel(idx_ref, idx_next_ref, src_ref, o_ref, buf_ref, sem, *, tg, ns, steps):
    i = pl.program_id(0)
    slot = i % 2

    def issue(ids_ref, s):
        def start(r, _):
            pltpu.make_async_copy(_slab(src_ref, ids_ref[0, 0, r], ns), _slab(buf_ref, s * tg + r, ns),
                                  sem.at[s]).start()
            return 0
        lax.fori_loop(0, tg, start, 0, unroll=ROW_DMA_UNROLL)

    @pl.when(i == 0)
    def _():
        issue(idx_ref, 0)

    @pl.when(i + 1 < steps)
    def _():
        issue(idx_next_ref, 1 - slot)

    base = pl.multiple_of(slot * (tg * ns), tg * ns)
    pltpu.make_async_copy(src_ref.at[pl.ds(0, tg * ns), :], buf_ref.at[pl.ds(base, tg * ns), :], sem.at[slot]).wait()
    half = ns * LANES
    for c in range(ns):
        words = buf_ref[pl.ds(base + c, tg, stride=ns), :]
        low = pltpu.bitcast(words << 16, F32)
        high = pltpu.bitcast(words & jnp.uint32(0xFFFF0000), F32)
        o_ref[:, c * LANES:(c + 1) * LANES] = low.astype(o_ref.dtype)
        o_ref[:, half + c * LANES:half + (c + 1) * LANES] = high.astype(o_ref.dtype)


def gather_rows(src, idx, ns, tg=512):
    n = idx.shape[0]
    tg = _tile(n, tg)
    steps = n // tg
    ids = idx.reshape(steps, 1, tg)
    return pl.pallas_call(
        functools.partial(_gather_rows_kernel, tg=tg, ns=ns, steps=steps),
        grid=(steps,),
        in_specs=[
            pl.BlockSpec((1, 1, tg), lambda m: (m, 0, 0), memory_space=pltpu.SMEM),
            pl.BlockSpec((1, 1, tg), lambda m: (jnp.minimum(m + 1, steps - 1), 0, 0), memory_space=pltpu.SMEM),
            pl.BlockSpec(memory_space=pl.ANY),
        ],
        out_specs=pl.BlockSpec((tg, 2 * ns * LANES), lambda m: (m, 0)),
        out_shape=jax.ShapeDtypeStruct((n, 2 * ns * LANES), BF16),
        scratch_shapes=[pltpu.VMEM((2 * tg * ns, LANES), src.dtype), pltpu.SemaphoreType.DMA((2,))],
        compiler_params=_params("arbitrary"),
        name="gather_rows",
    )(ids, ids, src)


def _gmm_swiglu_kernel(te_ref, nu_ref, a_ref, wg_ref, wu_ref, o_ref, wgb_ref, wub_ref):
    m = pl.program_id(1)
    used = m < nu_ref[0]
    fresh = jnp.logical_or(m == 0, te_ref[m] != te_ref[jnp.maximum(m - 1, 0)])

    @pl.when(fresh)
    def _():
        wgb_ref[...] = wg_ref[...].astype(BF16)
        wub_ref[...] = wu_ref[...].astype(BF16)

    @pl.when(used)
    def _():
        a = a_ref[...]
        g = jnp.dot(a, wgb_ref[...], preferred_element_type=F32)
        u = jnp.dot(a, wub_ref[...], preferred_element_type=F32)
        o_ref[...] = (g * jax.nn.sigmoid(g) * u).astype(o_ref.dtype)

    @pl.when(jnp.logical_not(used))
    def _():
        o_ref[...] = jnp.zeros_like(o_ref)


def gmm_swiglu(a, w_gu, layer, tile_expert, n_used, tm, tn=1024):
    ms, d = a.shape
    f = w_gu.shape[3] // 2
    tn = _tile(f, tn)
    nf = f // tn
    return pl.pallas_call(
        _gmm_swiglu_kernel,
        grid_spec=pltpu.PrefetchScalarGridSpec(
            num_scalar_prefetch=2,
            grid=(nf, ms // tm),
            in_specs=[
                pl.BlockSpec((tm, d), lambda j, m, te, nu: (m, 0)),
                pl.BlockSpec((None, None, d, tn), lambda j, m, te, nu: (layer, te[m], 0, j)),
                pl.BlockSpec((None, None, d, tn), lambda j, m, te, nu: (layer, te[m], 0, j + nf)),
            ],
            out_specs=pl.BlockSpec((tm, tn), lambda j, m, te, nu: (m, j)),
            scratch_shapes=[pltpu.VMEM((d, tn), BF16), pltpu.VMEM((d, tn), BF16)],
        ),
        out_shape=jax.ShapeDtypeStruct((ms, f), BF16),
        compiler_params=_params("parallel", "arbitrary"),
        name="gmm_swiglu",
    )(tile_expert, n_used, a, w_gu, w_gu)


def _gmm_kernel(te_ref, nu_ref, a_ref, w_ref, o_ref):
    used = pl.program_id(1) < nu_ref[0]

    @pl.when(used)
    def _():
        acc = jnp.dot(a_ref[...], w_ref[...], preferred_element_type=F32)
        for c in range(o_ref.shape[1]):
            o_ref[:, c, :] = acc[:, c * LANES:(c + 1) * LANES]

    @pl.when(jnp.logical_not(used))
    def _():
        o_ref[...] = jnp.zeros_like(o_ref)


def gmm(a, w, layer, tile_expert, n_used, tm, tn=1024):
    ms, k = a.shape
    n = w.shape[3]
    tn = _tile(n, tn, SUBLANES * LANES)
    return pl.pallas_call(
        _gmm_kernel,
        grid_spec=pltpu.PrefetchScalarGridSpec(
            num_scalar_prefetch=2,
            grid=(n // tn, ms // tm),
            in_specs=[
                pl.BlockSpec((tm, k), lambda j, m, te, nu: (m, 0)),
                pl.BlockSpec((None, None, k, tn), lambda j, m, te, nu: (layer, te[m], 0, j)),
            ],
            out_specs=pl.BlockSpec((tm, tn // LANES, LANES), lambda j, m, te, nu: (m, j, 0)),
        ),
        out_shape=jax.ShapeDtypeStruct((ms, n // LANES, LANES), F32),
        compiler_params=_params("parallel", "arbitrary"),
        name="gmm",
    )(tile_expert, n_used, a, w)


def _combine_ln_kernel(p0_ref, p1_ref, p0_next_ref, p1_next_ref, y_ref, gate_ref, x_ref, gmod_ref, lng_ref, lnb_ref,
                       o_ref, ybuf_ref, ys_ref, sem, *, tm, ns, steps, alpha):
    i = pl.program_id(0)
    slot = i % 2

    def issue(pa_ref, pb_ref, s):
        def start(r, _):
            row = (s * TOP_K) * tm + r
            pltpu.make_async_copy(_slab(y_ref, pa_ref[0, 0, r], ns), _slab(ybuf_ref, row, ns), sem.at[s]).start()
            pltpu.make_async_copy(_slab(y_ref, pb_ref[0, 0, r], ns), _slab(ybuf_ref, row + tm, ns), sem.at[s]).start()
            return 0
        lax.fori_loop(0, tm, start, 0, unroll=ROW_DMA_UNROLL)

    @pl.when(i == 0)
    def _():
        issue(p0_ref, p1_ref, 0)

    @pl.when(i + 1 < steps)
    def _():
        issue(p0_next_ref, p1_next_ref, 1 - slot)

    rows = TOP_K * tm * ns
    base = pl.multiple_of(slot * rows, rows)
    pltpu.make_async_copy(y_ref.at[pl.ds(0, rows), :], ybuf_ref.at[pl.ds(base, rows), :], sem.at[slot]).wait()
    gates = gate_ref[...]
    g0, g1 = gates[:, 0:1], gates[:, 1:2]
    for c in range(ns):
        y0 = ybuf_ref[pl.ds(base + c, tm, stride=ns), :]
        y1 = ybuf_ref[pl.ds(base + tm * ns + c, tm, stride=ns), :]
        ys_ref[:, c * LANES:(c + 1) * LANES] = g0 * y0 + g1 * y1
    o_ref[...] = _res_ln(x_ref[...], ys_ref[...], gmod_ref[0], lng_ref[...], lnb_ref[...], alpha)


def combine_ln(y_slabs, pos0, pos1, gates, x, modr, layer, nb, ln_g, ln_b, alpha, tm=256):
    t, d = x.shape
    tm = _tile(t // nb, tm)
    tpb = (t // nb) // tm
    nt = t // tm
    ns = d // LANES
    ids_spec = pl.BlockSpec((1, 1, tm), lambda m: (m, 0, 0), memory_space=pltpu.SMEM)
    ids_next_spec = pl.BlockSpec((1, 1, tm), lambda m: (jnp.minimum(m + 1, nt - 1), 0, 0), memory_space=pltpu.SMEM)
    p0, p1 = pos0.reshape(nt, 1, tm), pos1.reshape(nt, 1, tm)
    return pl.pallas_call(
        functools.partial(_combine_ln_kernel, tm=tm, ns=ns, steps=nt, alpha=alpha),
        grid=(nt,),
        in_specs=[
            ids_spec,
            ids_spec,
            ids_next_spec,
            ids_next_spec,
            pl.BlockSpec(memory_space=pl.ANY),
            pl.BlockSpec((tm, LANES), lambda m: (m, 0)),
            pl.BlockSpec((tm, d), lambda m: (m, 0)),
            _mod_spec(d, layer, nb, tpb, 5),
            pl.BlockSpec((1, d), lambda m: (0, 0)),
            pl.BlockSpec((1, d), lambda m: (0, 0)),
        ],
        out_specs=pl.BlockSpec((tm, d), lambda m: (m, 0)),
        out_shape=jax.ShapeDtypeStruct((t, d), F32),
        scratch_shapes=[pltpu.VMEM((2 * TOP_K * tm * ns, LANES), F32), pltpu.VMEM((tm, d), F32),
                        pltpu.SemaphoreType.DMA((2,))],
        compiler_params=_params("arbitrary"),
        name="combine_ln",
    )(p0, p1, p0, p1, y_slabs.reshape(-1, LANES), gates, x, modr, ln_g.reshape(1, d), ln_b.reshape(1, d))


def moe_layer(x, modr, layer, nb, w_router, w_gu, w_down, moe_index, ln_g, ln_b, alpha, tm_g=512):
    t, d = x.shape
    n_exp = w_router.shape[1]
    h, info, gates, cnt = router(x, modr, layer, nb, w_router)
    counts = cnt[0, :n_exp].astype(jnp.int32)
    sizes = (counts + tm_g - 1) // tm_g * tm_g
    ends = jnp.cumsum(sizes)
    starts = ends - sizes
    pos0 = starts[info[:, 0]] + info[:, 2]
    pos1 = starts[info[:, 1]] + info[:, 3]
    ms = t * TOP_K + n_exp * tm_g
    tok = jnp.arange(t, dtype=jnp.int32)
    slot_token = jnp.zeros((ms,), jnp.int32).at[jnp.concatenate([pos0, pos1])].set(jnp.concatenate([tok, tok]))
    tile_start = jnp.arange(ms // tm_g, dtype=jnp.int32) * tm_g
    tile_expert = jnp.minimum(jnp.sum(tile_start[:, None] >= ends[None, :], axis=1), n_exp - 1).astype(jnp.int32)
    n_used = (ends[-1:] // tm_g).astype(jnp.int32)

    h_sorted = gather_rows(h, slot_token, d // (2 * LANES))
    act = gmm_swiglu(h_sorted, w_gu, moe_index, tile_expert, n_used, tm_g)
    y_slabs = gmm(act, w_down, moe_index, tile_expert, n_used, tm_g)
    return combine_ln(y_slabs, pos0, pos1, gates, x, modr, layer, nb, ln_g, ln_b, alpha)


def kernel(x, c, positions, mod_w, mod_b, ln_g, ln_b, sb_w_qkv, sb_w_o, mla_w_in, mla_q_norm, mla_kv_norm,
           mla_w_uq, mla_w_ukv, mla_w_o, ffn_w_gu, ffn_w_down, moe_router, moe_w_gu, moe_w_down):
    nb, s, d = x.shape
    depth = mod_w.shape[0]
    t = nb * s
    alpha = float((2 * depth) ** 0.25)
    xf = x.reshape(t, d)

    modr = mod_table(c, mod_w, mod_b)
    cos, sin = rope_tables(positions)

    sb_heads = sb_w_o.shape[1] // HEAD_DIM
    sb_scale = HEAD_DIM ** -0.5 * LOG2_E
    sb_col_scale = jnp.concatenate([jnp.full((sb_heads * HEAD_DIM,), sb_scale, F32),
                                    jnp.ones((2 * sb_heads * HEAD_DIM,), F32)]).reshape(1, -1)

    mla_heads = mla_w_o.shape[1] // HEAD_DIM
    q_rank = mla_q_norm.shape[1]
    mla_scale = float((HEAD_DIM + ROPE_DIM) ** -0.5) * LOG2_E
    moe_w_down_bf16 = moe_w_down.astype(BF16)

    for i in range(depth):
        j = i // 2
        if i % 2 == 0:
            n_qk = 2 * sb_heads * HEAD_DIM
            qk = modmm(xf, modr, i, nb, 0, sb_w_qkv[j, :, :n_qk].astype(BF16), sb_col_scale[:, :n_qk])
            vt = modmm_t(xf, modr, i, nb, 0, sb_w_qkv[j, :, n_qk:].T.astype(BF16), _tile(s, SB_KEY_BLOCK))
            o = sb_attention(qk, vt, nb)
            w_o = sb_w_o[j]
        else:
            w_in = jnp.pad(mla_w_in[j], ((0, 0), (0, LANES - ROPE_DIM))).astype(BF16)
            cq, ckv, kr = mla_in(xf, modr, i, nb, w_in, mla_q_norm[j], mla_kv_norm[j], cos, sin)
            w_uq = mla_w_uq[j].reshape(q_rank, mla_heads, HEAD_DIM + ROPE_DIM)
            w_uq = jnp.pad(w_uq, ((0, 0), (0, 0), (0, LANES - ROPE_DIM))).reshape(q_rank, mla_heads * 2 * LANES)
            q = mm_q_rope(cq, w_uq.astype(BF16), cos, sin, mla_scale)
            w_ukv = mla_w_ukv[j].reshape(-1, mla_heads, 2 * HEAD_DIM)
            w_uk = w_ukv[:, :, :HEAD_DIM].reshape(-1, mla_heads * HEAD_DIM).astype(BF16)
            w_uv_t = w_ukv[:, :, HEAD_DIM:].reshape(-1, mla_heads * HEAD_DIM).T.astype(BF16)
            kn = mm(ckv, w_uk)
            vt = mm_t(ckv, w_uv_t, _tile(s, ATTN_BLOCK))
            o = mla_attention(q, kn, kr, vt, nb)
            w_o = mla_w_o[j]
        xf = mm_res_ln(o, w_o.astype(BF16), xf, modr, i, nb, 2, ln_g[i, 0], ln_b[i, 0], alpha)
        if i % 2 == 0:
            act = modmm_swiglu(xf, modr, i, nb, 3, ffn_w_gu[j].astype(BF16))
            xf = mm_res_ln(act, ffn_w_down[j].astype(BF16), xf, modr, i, nb, 5, ln_g[i, 1], ln_b[i, 1], alpha)
        else:
            xf = moe_layer(xf, modr, i, nb, moe_router[j], moe_w_gu, moe_w_down_bf16, j,
                           ln_g[i, 1], ln_b[i, 1], alpha)
    return xf.reshape(nb, s, d)
```
